```python
import jax, jax.numpy as jnp
from jax import lax
import numpy as np

D_MODEL = 1024
BATCH = 8
SEQ = 8192
DEPTH = 1

HEAD_DIM = 64
N_FOX_HEADS = 8
N_SB_HEADS = 8
FOX_WIDTH = N_FOX_HEADS * HEAD_DIM
SB_WIDTH = N_SB_HEADS * HEAD_DIM
IN_COLS = 3 * FOX_WIDTH + N_FOX_HEADS + 3 * SB_WIDTH + 2 * D_MODEL
D_FF = 2816
CONV_WIDTH = 3
Q_BLOCK = 128
LN_EPS = 1e-5
N_MOD = 6
DEEPNORM_ALPHA = (2.0 * DEPTH) ** 0.25
DEEPNORM_BETA = (8.0 * DEPTH) ** -0.25
ATTN_SCALE = HEAD_DIM ** -0.5

kernel_name = "fox_stickbreak_gated_hybrid_deepnorm_adaln"


def _split_points():
    cols = [FOX_WIDTH] * 3 + [N_FOX_HEADS] + [SB_WIDTH] * 3 + [D_MODEL] * 2
    return [int(v) for v in np.cumsum(cols)[:-1]]


def _layer_norm(x, g, b):
    xf = x.astype(jnp.float32)
    mu = jnp.mean(xf, axis=-1, keepdims=True)
    var = jnp.mean(jnp.square(xf - mu), axis=-1, keepdims=True)
    y = (xf - mu) * lax.rsqrt(var + LN_EPS)
    return (y * g.astype(jnp.float32) + b.astype(jnp.float32)).astype(x.dtype)


def _to_heads(t, n_heads):
    b, s, _ = t.shape
    return t.reshape(b, s, n_heads, HEAD_DIM).transpose(0, 2, 1, 3)


def _merge_heads(t):
    b, h, s, d = t.shape
    return t.transpose(0, 2, 1, 3).reshape(b, s, h * d)


def _blocks(t):
    b, h, s = t.shape[:3]
    nb = s // Q_BLOCK
    t = t.reshape((b, h, nb, Q_BLOCK) + t.shape[3:])
    return jnp.moveaxis(t, 2, 0)


def _unblocks(t):
    nb, b, h, qb, d = t.shape
    return jnp.moveaxis(t, 0, 2).reshape(b, h, nb * qb, d)


def _forgetting_attention(q, k, v, log_f):
    s_len = q.shape[2]
    cum = jnp.cumsum(log_f, axis=-1)
    kpos = jnp.arange(s_len)

    def block(args):
        qi, cqi, bi = args
        qpos = bi * Q_BLOCK + jnp.arange(Q_BLOCK)
        logits = (jnp.einsum('bhqd,bhkd->bhqk', qi, k).astype(jnp.float32) * ATTN_SCALE
                  + cqi[..., None] - cum[:, :, None, :])
        causal = kpos[None, :] <= qpos[:, None]
        logits = jnp.where(causal, logits, -jnp.inf)
        p = jax.nn.softmax(logits, axis=-1)
        return jnp.einsum('bhqk,bhkd->bhqd', p.astype(v.dtype), v)

    nb = s_len // Q_BLOCK
    out = lax.map(block, (_blocks(q), _blocks(cum), jnp.arange(nb, dtype=jnp.int32)))
    return _unblocks(out)


def _stick_breaking_attention(q, k, v):
    s_len = q.shape[2]
    kpos = jnp.arange(s_len)

    def block(args):
        qi, bi = args
        qpos = bi * Q_BLOCK + jnp.arange(Q_BLOCK)
        z = jnp.einsum('bhqd,bhkd->bhqk', qi, k).astype(jnp.float32) * ATTN_SCALE
        strict = kpos[None, :] < qpos[:, None]
        log_beta = jax.nn.log_sigmoid(z)
        log_one_minus = jnp.where(strict, jax.nn.log_sigmoid(-z), 0.0)
        rest = lax.cumsum(log_one_minus, axis=3, reverse=True) - log_one_minus
        a = jnp.where(strict, jnp.exp(log_beta + rest), 0.0)
        return jnp.einsum('bhqk,bhkd->bhqd', a.astype(v.dtype), v)

    nb = s_len // Q_BLOCK
    out = lax.map(block, (_blocks(q), jnp.arange(nb, dtype=jnp.int32)))
    return _unblocks(out)


def _causal_depthwise_conv(h, w, b):
    s_len = h.shape[1]
    hp = jnp.pad(h, ((0, 0), (CONV_WIDTH - 1, 0), (0, 0)))
    out = b
    for tap in range(CONV_WIDTH):
        out = out + hp[:, tap:tap + s_len, :] * w[tap]
    return out


def setup_inputs(seed: int = 0) -> dict:
    key = jax.random.key(seed)
    ks = jax.random.split(key, 17)
    f32 = jnp.float32
    d = D_MODEL
    nrm = lambda k, shape, s: jax.random.normal(k, shape, f32) * s
    return {
        "x": nrm(ks[0], (BATCH, SEQ, d), 1.0),
        "c": nrm(ks[1], (BATCH, d), 1.0),
        "w_ada": nrm(ks[2], (d, N_MOD * d), 0.1 * d ** -0.5),
        "b_ada": nrm(ks[3], (N_MOD * d,), 0.01),
        "w_in": nrm(ks[4], (d, IN_COLS), d ** -0.5),
        "b_forget": jnp.linspace(1.0, 5.0, N_FOX_HEADS, dtype=f32) + nrm(ks[5], (N_FOX_HEADS,), 0.1),
        "w_fox_proj": nrm(ks[6], (FOX_WIDTH, d), FOX_WIDTH ** -0.5),
        "w_sb_proj": nrm(ks[7], (SB_WIDTH, d), SB_WIDTH ** -0.5),
        "w_o": nrm(ks[8], (d, d), DEEPNORM_BETA * d ** -0.5),
        "ln1_g": 1.0 + nrm(ks[9], (d,), 0.02),
        "ln1_b": nrm(ks[10], (d,), 0.02),
        "w_up": nrm(ks[11], (d, 2 * D_FF), d ** -0.5),
        "conv_w": nrm(ks[12], (CONV_WIDTH, 2 * D_FF), CONV_WIDTH ** -0.5),
        "conv_b": nrm(ks[13], (2 * D_FF,), 0.01),
        "w_down": nrm(ks[14], (D_FF, d), DEEPNORM_BETA * D_FF ** -0.5),
        "ln2_g": 1.0 + nrm(ks[15], (d,), 0.02),
        "ln2_b": nrm(ks[16], (d,), 0.02),
    }


def reference(x, c, w_ada, b_ada, w_in, b_forget, w_fox_proj, w_sb_proj, w_o,
              ln1_g, ln1_b, w_up, conv_w, conv_b, w_down, ln2_g, ln2_b):
    for _ in range(DEPTH):
        mod = c @ w_ada + b_ada
        sh1, sc1, gt1, sh2, sc2, gt2 = [m[:, None, :] for m in jnp.split(mod, N_MOD, axis=-1)]

        u = x * (1.0 + sc1) + sh1
        proj = u @ w_in
        q_a, k_a, v_a, f_a, q_b, k_b, v_b, g_a, g_b = jnp.split(proj, _split_points(), axis=-1)

        log_f = jax.nn.log_sigmoid((f_a + b_forget).astype(jnp.float32)).transpose(0, 2, 1)
        y_fox = _merge_heads(_forgetting_attention(
            _to_heads(q_a, N_FOX_HEADS), _to_heads(k_a, N_FOX_HEADS), _to_heads(v_a, N_FOX_HEADS), log_f))
        y_sb = _merge_heads(_stick_breaking_attention(
            _to_heads(q_b, N_SB_HEADS), _to_heads(k_b, N_SB_HEADS), _to_heads(v_b, N_SB_HEADS)))

        merged = jax.nn.sigmoid(g_a) * (y_fox @ w_fox_proj) + jax.nn.sigmoid(g_b) * (y_sb @ w_sb_proj)
        attn_out = merged @ w_o
        x = _layer_norm(DEEPNORM_ALPHA * x + (1.0 + gt1) * attn_out, ln1_g, ln1_b)

        u2 = x * (1.0 + sc2) + sh2
        h = _causal_depthwise_conv(u2 @ w_up, conv_w, conv_b)
        h_gate, h_val = jnp.split(h, 2, axis=-1)
        ffn_out = (jax.nn.silu(h_gate) * h_val) @ w_down
        x = _layer_norm(DEEPNORM_ALPHA * x + (1.0 + gt2) * ffn_out, ln2_g, ln2_b)
    return x
```

```python
import functools

import jax
import jax.numpy as jnp
from jax import lax
from jax.experimental import pallas as pl
from jax.experimental.pallas import tpu as pltpu

F32 = jnp.float32
BF16 = jnp.bfloat16

HEAD_DIM = 64
N_HEADS = 8
ATTN_WIDTH = N_HEADS * HEAD_DIM
N_MOD = 6
CONV_WIDTH = 3
LN_EPS = 1e-5
DEPTH = 1
DEEPNORM_ALPHA = (2.0 * DEPTH) ** 0.25
ATTN_SCALE = HEAD_DIM ** -0.5

LANES = 128
SUBLANES = 8
HEADS_PER_VREG_ROW = LANES // HEAD_DIM
VMEM_LIMIT = 56 * 1024 * 1024

EXP_ZERO_BELOW = -110.0


def _cparams(n_axes):
    return pltpu.CompilerParams(
        dimension_semantics=("arbitrary",) * n_axes, vmem_limit_bytes=VMEM_LIMIT)


def _resident(shape):
    nd = len(shape)
    return pl.BlockSpec(shape, lambda *_: (0,) * nd, pipeline_mode=pl.Buffered(1))


def _log_sigmoid(x):
    return jnp.minimum(x, 0.0) - jnp.log1p(jnp.exp(-jnp.abs(x)))


def _layer_norm(y, g, b):
    mu = jnp.mean(y, axis=-1, keepdims=True)
    d = y - mu
    var = jnp.mean(d * d, axis=-1, keepdims=True)
    return d * lax.rsqrt(var + LN_EPS) * g + b


def _ada_kernel(c_ref, w_ref, b_ref, o_ref):
    o_ref[...] = jnp.dot(c_ref[...], w_ref[...], preferred_element_type=F32,
                         precision=lax.Precision.HIGHEST) + b_ref[...]


def _ada(c, w_ada, b_ada):
    bsz, d = c.shape
    n = w_ada.shape[1]
    tn = n // 4
    return pl.pallas_call(
        _ada_kernel,
        grid=(n // tn,),
        in_specs=[pl.BlockSpec((bsz, d), lambda j: (0, 0)),
                  pl.BlockSpec((d, tn), lambda j: (0, j)),
                  pl.BlockSpec((1, tn), lambda j: (0, j))],
        out_specs=pl.BlockSpec((bsz, tn), lambda j: (0, j)),
        out_shape=jax.ShapeDtypeStruct((bsz, n), F32),
        compiler_params=_cparams(1),
        name="adaln_mod",
    )(c, w_ada, b_ada.reshape(1, n))


def _inproj_kernel(x_ref, mod_ref, wa_ref, wb_ref, wf_ref, bf_ref, qkva_ref, qkvb_ref, logf_ref):
    x = x_ref[0]
    sh = mod_ref[0, 0:1, :]
    sc = mod_ref[0, 1:2, :]
    u = (x * (1.0 + sc) + sh).astype(BF16)
    qkva_ref[0] = jnp.dot(u, wa_ref[...], preferred_element_type=F32).astype(BF16)
    qkvb_ref[0] = jnp.dot(u, wb_ref[...], preferred_element_type=F32).astype(BF16)
    f = jnp.dot(u, wf_ref[...], preferred_element_type=F32)
    logf = _log_sigmoid(f + bf_ref[...])
    logf_ref[0] = jnp.transpose(logf)[0:N_HEADS, :]


def _inproj(x, mod, wa, wb, wf, bfg, tm):
    bsz, s, d = x.shape
    na = wa.shape[1]
    return pl.pallas_call(
        _inproj_kernel,
        grid=(bsz, s // tm),
        in_specs=[pl.BlockSpec((1, tm, d), lambda b, i: (b, i, 0)),
                  pl.BlockSpec((1, N_MOD, d), lambda b, i: (b, 0, 0)),
                  _resident(wa.shape), _resident(wb.shape), _resident(wf.shape),
                  _resident(bfg.shape)],
        out_specs=[pl.BlockSpec((1, tm, na), lambda b, i: (b, i, 0)),
                   pl.BlockSpec((1, tm, na), lambda b, i: (b, i, 0)),
                   pl.BlockSpec((1, N_HEADS, tm), lambda b, i: (b, 0, i))],
        out_shape=[jax.ShapeDtypeStruct((bsz, s, na), BF16),
                   jax.ShapeDtypeStruct((bsz, s, na), BF16),
                   jax.ShapeDtypeStruct((bsz, N_HEADS, s), F32)],
        compiler_params=_cparams(2),
        name="in_proj",
    )(x, mod, wa, wb, wf, bfg)


def _split3(x):
    hi = x.astype(BF16)
    r = x - hi.astype(F32)
    mid = r.astype(BF16)
    lo = (r - mid.astype(F32)).astype(BF16)
    return hi, mid, lo


def _cumsum_kernel(x_ref, tri_ref, o_ref, *, chunk):
    rows, s = x_ref.shape
    carry = jnp.zeros((rows, 1), F32)
    for c in range(s // chunk):
        xc = x_ref[:, c * chunk:(c + 1) * chunk]
        cat = jnp.concatenate(_split3(xc), axis=1)
        out = jnp.dot(cat, tri_ref[...], preferred_element_type=F32) + carry
        o_ref[:, c * chunk:(c + 1) * chunk] = out
        carry = out[:, chunk - 1:chunk]


def _cumsum_rows(x, chunk):
    rows, s = x.shape
    j = lax.broadcasted_iota(jnp.int32, (3 * chunk, chunk), 0) % chunk
    t = lax.broadcasted_iota(jnp.int32, (3 * chunk, chunk), 1)
    tri = (j <= t).astype(BF16)
    return pl.pallas_call(
        functools.partial(_cumsum_kernel, chunk=chunk),
        grid=(1,),
        in_specs=[pl.BlockSpec((rows, s), lambda i: (0, 0)),
                  pl.BlockSpec(tri.shape, lambda i: (0, 0))],
        out_specs=pl.BlockSpec((rows, s), lambda i: (0, 0)),
        out_shape=jax.ShapeDtypeStruct((rows, s), F32),
        compiler_params=_cparams(1),
        name="forget_cumsum",
    )(x, tri)


def _fox_kernel(q_ref, k_ref, v_ref, cum_ref, o_ref, m_ref, l_ref, acc_ref, *, t):
    p = pl.program_id(1)
    i = pl.program_id(2)
    q = q_ref[0]
    lane = lax.broadcasted_iota(jnp.int32, (t, LANES), 1)
    zero = jnp.zeros_like(q)
    qs = (jnp.where(lane < HEAD_DIM, q, zero), jnp.where(lane >= HEAD_DIM, q, zero))
    row0 = pl.multiple_of(i * t, t)
    crefs = [cum_ref[0, pl.ds(HEADS_PER_VREG_ROW * p + r, 1), pl.ds(row0, t)][:, 0:1]
             for r in range(HEADS_PER_VREG_ROW)]

    m_ref[...] = jnp.full(m_ref.shape, -jnp.inf, F32)
    l_ref[...] = jnp.zeros(l_ref.shape, F32)
    acc_ref[...] = jnp.zeros(acc_ref.shape, F32)

    def step(j, diagonal):
        col0 = pl.multiple_of(j * t, t)
        ks = k_ref[0, pl.ds(col0, t), :]
        vs = v_ref[0, pl.ds(col0, t), :]
        for r in range(HEADS_PER_VREG_ROW):
            cs = cum_ref[0, pl.ds(HEADS_PER_VREG_ROW * p + r, 1), pl.ds(col0, t)]
            s = lax.dot_general(qs[r], ks, (((1,), (1,)), ((), ())), preferred_element_type=F32)
            s = s + (crefs[r] - cs)
            if diagonal:
                rr = lax.broadcasted_iota(jnp.int32, (t, t), 0)
                cc = lax.broadcasted_iota(jnp.int32, (t, t), 1)
                s = jnp.where(cc <= rr, s, -jnp.inf)
            m_prev = m_ref[r]
            m_new = jnp.maximum(m_prev, jnp.max(s, axis=1, keepdims=True))
            alpha = jnp.exp(m_prev - m_new)
            pe = jnp.exp(s - m_new)
            l_ref[r] = alpha * l_ref[r] + jnp.sum(pe, axis=1, keepdims=True)
            acc_ref[r] = alpha * acc_ref[r] + jnp.dot(pe.astype(BF16), vs, preferred_element_type=F32)
            m_ref[r] = m_new

    def body(j, carry):
        step(j, False)
        return carry

    lax.fori_loop(0, i, body, 0)
    step(i, True)

    outs = [acc_ref[r] / l_ref[r] for r in range(HEADS_PER_VREG_ROW)]
    o_ref[0] = jnp.where(lane < HEAD_DIM, outs[0], outs[1]).astype(o_ref.dtype)


def _fox_attention(qkv, cum, t):
    bsz, s, _ = qkv.shape
    n_pairs = N_HEADS // HEADS_PER_VREG_ROW
    return pl.pallas_call(
        functools.partial(_fox_kernel, t=t),
        grid=(bsz, n_pairs, s // t),
        in_specs=[pl.BlockSpec((1, t, LANES), lambda b, p, i: (b, i, p)),
                  pl.BlockSpec((1, s, LANES), lambda b, p, i: (b, 0, n_pairs + p)),
                  pl.BlockSpec((1, s, LANES), lambda b, p, i: (b, 0, 2 * n_pairs + p)),
                  pl.BlockSpec((1, N_HEADS, s), lambda b, p, i: (b, 0, 0))],
        out_specs=pl.BlockSpec((1, t, LANES), lambda b, p, i: (b, i, p)),
        out_shape=jax.ShapeDtypeStruct((bsz, s, ATTN_WIDTH), BF16),
        scratch_shapes=[pltpu.VMEM((HEADS_PER_VREG_ROW, t, 1), F32),
                        pltpu.VMEM((HEADS_PER_VREG_ROW, t, 1), F32),
                        pltpu.VMEM((HEADS_PER_VREG_ROW, t, LANES), F32)],
        compiler_params=_cparams(3),
        name="fox_attention",
    )(qkv, qkv, qkv, cum)


def _sb_kernel(q_ref, k_ref, v_ref, u_ref, o_ref, acc_ref, carry_ref, *, t, heads):
    i = pl.program_id(2)
    n_pairs = heads // HEADS_PER_VREG_ROW
    lane = lax.broadcasted_iota(jnp.int32, (t, LANES), 1)
    qs = []
    for p in range(n_pairs):
        q = q_ref[0, :, p * LANES:(p + 1) * LANES]
        zero = jnp.zeros_like(q)
        qs.append((jnp.where(lane < HEAD_DIM, q, zero), jnp.where(lane >= HEAD_DIM, q, zero)))

    acc_ref[...] = jnp.zeros(acc_ref.shape, F32)
    carry_ref[...] = jnp.zeros(carry_ref.shape, F32)

    def step(j, diagonal):
        col0 = pl.multiple_of(j * t, t)
        kb = k_ref[0, pl.ds(col0, t), :]
        vb = v_ref[0, pl.ds(col0, t), :]
        worst = None
        for p in range(n_pairs):
            ks = kb[:, p * LANES:(p + 1) * LANES]
            vs = vb[:, p * LANES:(p + 1) * LANES]
            for r in range(HEADS_PER_VREG_ROW):
                h = HEADS_PER_VREG_ROW * p + r
                z = lax.dot_general(qs[p][r], ks, (((1,), (1,)), ((), ())), preferred_element_type=F32)
                sp = jnp.maximum(z, 0.0) + jnp.log1p(jnp.exp(-jnp.abs(z)))
                if diagonal:
                    rr = lax.broadcasted_iota(jnp.int32, (t, t), 0)
                    cc = lax.broadcasted_iota(jnp.int32, (t, t), 1)
                    strict = cc < rr
                    sp = jnp.where(strict, sp, 0.0)
                hi = sp.astype(BF16)
                lo = (sp - hi.astype(F32)).astype(BF16)
                red = jnp.dot(jnp.concatenate([hi, lo], axis=1), u_ref[...], preferred_element_type=F32)
                carry = carry_ref[h]
                a = jnp.exp(z + red[:, :t] + carry)
                if diagonal:
                    a = jnp.where(strict, a, 0.0)
                acc_ref[h] += jnp.dot(a.astype(BF16), vs, preferred_element_type=F32)
                carry = carry + red[:, t:]
                carry_ref[h] = carry
                worst = carry if worst is None else jnp.maximum(worst, carry)
        return jnp.max(worst)

    worst0 = step(i, True)

    def cond(state):
        j, worst = state
        return jnp.logical_and(j >= 0, worst >= EXP_ZERO_BELOW)

    def body(state):
        j, _ = state
        return j - 1, step(j, False)

    lax.while_loop(cond, body, (i - 1, worst0))

    for p in range(n_pairs):
        o_ref[0, :, p * LANES:(p + 1) * LANES] = jnp.where(
            lane < HEAD_DIM, acc_ref[HEADS_PER_VREG_ROW * p], acc_ref[HEADS_PER_VREG_ROW * p + 1]
        ).astype(o_ref.dtype)


def _sb_attention(qkv, t, heads):
    bsz, s, _ = qkv.shape
    n_groups = N_HEADS // heads
    w = heads * HEAD_DIM
    j = lax.broadcasted_iota(jnp.int32, (2 * t, 2 * t), 0) % t
    c = lax.broadcasted_iota(jnp.int32, (2 * t, 2 * t), 1)
    u = -jnp.logical_or(c >= t, j >= c).astype(BF16)
    return pl.pallas_call(
        functools.partial(_sb_kernel, t=t, heads=heads),
        grid=(bsz, n_groups, s // t),
        in_specs=[pl.BlockSpec((1, t, w), lambda b, g, i: (b, i, g)),
                  pl.BlockSpec((1, s, w), lambda b, g, i: (b, 0, n_groups + g)),
                  pl.BlockSpec((1, s, w), lambda b, g, i: (b, 0, 2 * n_groups + g)),
                  pl.BlockSpec(u.shape, lambda b, g, i: (0, 0))],
        out_specs=pl.BlockSpec((1, t, w), lambda b, g, i: (b, i, g)),
        out_shape=jax.ShapeDtypeStruct((bsz, s, ATTN_WIDTH), BF16),
        scratch_shapes=[pltpu.VMEM((heads, t, LANES), F32),
                        pltpu.VMEM((heads, t, LANES), F32)],
        compiler_params=_cparams(3),
        name="stickbreak_attention",
    )(qkv, qkv, qkv, u)


def _outproj_kernel(x_ref, mod_ref, yf_ref, ys_ref, wg_ref, wfp_ref, wsp_ref, wo_ref, g_ref, b_ref, o_ref):
    x = x_ref[0]
    d = x.shape[-1]
    sh = mod_ref[0, 0:1, :]
    sc = mod_ref[0, 1:2, :]
    gt = mod_ref[0, 2:3, :]
    u = (x * (1.0 + sc) + sh).astype(BF16)
    gates = jnp.dot(u, wg_ref[...], preferred_element_type=F32)
    pa = jnp.dot(yf_ref[0], wfp_ref[...], preferred_element_type=F32)
    pb = jnp.dot(ys_ref[0], wsp_ref[...], preferred_element_type=F32)
    merged = jax.nn.sigmoid(gates[:, :d]) * pa + jax.nn.sigmoid(gates[:, d:]) * pb
    attn = jnp.dot(merged.astype(BF16), wo_ref[...], preferred_element_type=F32)
    y = DEEPNORM_ALPHA * x + (1.0 + gt) * attn
    o_ref[0] = _layer_norm(y, g_ref[...], b_ref[...])


def _outproj(x, mod, yf, ys, wg, wfp, wsp, wo, g, b, tm):
    bsz, s, d = x.shape
    aw = yf.shape[-1]
    return pl.pallas_call(
        _outproj_kernel,
        grid=(bsz, s // tm),
        in_specs=[pl.BlockSpec((1, tm, d), lambda b_, i: (b_, i, 0)),
                  pl.BlockSpec((1, N_MOD, d), lambda b_, i: (b_, 0, 0)),
                  pl.BlockSpec((1, tm, aw), lambda b_, i: (b_, i, 0)),
                  pl.BlockSpec((1, tm, aw), lambda b_, i: (b_, i, 0)),
                  _resident(wg.shape), _resident(wfp.shape), _resident(wsp.shape), _resident(wo.shape),
                  _resident(g.shape), _resident(b.shape)],
        out_specs=pl.BlockSpec((1, tm, d), lambda b_, i: (b_, i, 0)),
        out_shape=jax.ShapeDtypeStruct((bsz, s, d), F32),
        compiler_params=_cparams(2),
        name="merge_out_proj_ln",
    )(x, mod, yf, ys, wg, wfp, wsp, wo, g, b)


def _ffn_kernel(x_ref, mod_ref, wup_ref, cw_ref, cb_ref, wdn_ref, g_ref, b_ref, o_ref,
                tail_ref, hbuf_ref, *, tm, fc):
    i = pl.program_id(1)
    d_ff = wdn_ref.shape[0]
    halo = SUBLANES

    @pl.when(i == 0)
    def _():
        tail_ref[...] = jnp.zeros(tail_ref.shape, F32)

    x = x_ref[0]
    sh = mod_ref[0, 3:4, :]
    sc = mod_ref[0, 4:5, :]
    gt = mod_ref[0, 5:6, :]
    u = (x * (1.0 + sc) + sh).astype(BF16)

    def conv_part(col0):
        h = jnp.dot(u, wup_ref[:, col0:col0 + fc], preferred_element_type=F32)
        hbuf_ref[0:halo, :] = tail_ref[:, col0:col0 + fc]
        hbuf_ref[halo:halo + tm, :] = h
        tail_ref[:, col0:col0 + fc] = h[tm - halo:tm, :]
        h1 = hbuf_ref[halo - 1:halo - 1 + tm, :]
        h2 = hbuf_ref[halo - 2:halo - 2 + tm, :]
        w = cw_ref[:, col0:col0 + fc]
        out = cb_ref[:, col0:col0 + fc] + h2 * w[0:1, :]
        out = out + h1 * w[1:2, :]
        return out + h * w[2:3, :]

    acc = jnp.zeros(x.shape, F32)
    for c in range(d_ff // fc):
        hg = conv_part(c * fc)
        hv = conv_part(d_ff + c * fc)
        act = (hg * jax.nn.sigmoid(hg)) * hv
        acc = acc + jnp.dot(act.astype(BF16), wdn_ref[c * fc:(c + 1) * fc, :], preferred_element_type=F32)

    y = DEEPNORM_ALPHA * x + (1.0 + gt) * acc
    o_ref[0] = _layer_norm(y, g_ref[...], b_ref[...])


def _ffn(x, mod, wup, cw, cb, wdn, g, b, tm, fc):
    bsz, s, d = x.shape
    f2 = wup.shape[1]
    return pl.pallas_call(
        functools.partial(_ffn_kernel, tm=tm, fc=fc),
        grid=(bsz, s // tm),
        in_specs=[pl.BlockSpec((1, tm, d), lambda b_, i: (b_, i, 0)),
                  pl.BlockSpec((1, N_MOD, d), lambda b_, i: (b_, 0, 0)),
                  _resident(wup.shape), _resident(cw.shape), _resident(cb.shape), _resident(wdn.shape),
                  _resident(g.shape), _resident(b.shape)],
        out_specs=pl.BlockSpec((1, tm, d), lambda b_, i: (b_, i, 0)),
        out_shape=jax.ShapeDtypeStruct((bsz, s, d), F32),
        scratch_shapes=[pltpu.VMEM((SUBLANES, f2), F32),
                        pltpu.VMEM((SUBLANES + tm, fc), F32)],
        compiler_params=_cparams(2),
        name="conv_ffn_ln",
    )(x, mod, wup, cw, cb, wdn, g, b)


def _tiles(s):
    return dict(
        proj_rows=min(512, s),
        attn_block=min(512, s),
        sb_block=min(128, s),
        sb_heads=N_HEADS,
        cumsum_chunk=min(512, s),
        ffn_rows=min(512, s),
    )


def kernel(x, c, w_ada, b_ada, w_in, b_forget, w_fox_proj, w_sb_proj, w_o, ln1_g, ln1_b,
           w_up, conv_w, conv_b, w_down, ln2_g, ln2_b):
    bsz, s, d = x.shape
    d_ff = w_down.shape[0]
    tl = _tiles(s)
    aw = ATTN_WIDTH

    qscale = jnp.concatenate([jnp.full((aw,), ATTN_SCALE, F32), jnp.ones((2 * aw,), F32)])
    wa = (w_in[:, 0:3 * aw] * qscale).astype(BF16)
    wf = jnp.pad(w_in[:, 3 * aw:3 * aw + N_HEADS], ((0, 0), (0, LANES - N_HEADS))).astype(BF16)
    b0 = 3 * aw + N_HEADS
    wb = (w_in[:, b0:b0 + 3 * aw] * qscale).astype(BF16)
    wg = w_in[:, b0 + 3 * aw:].astype(BF16)

    for _ in range(DEPTH):
        mod = _ada(c, w_ada, b_ada).reshape(bsz, N_MOD, d)
        qkva, qkvb, logf = _inproj(x, mod, wa, wb, wf, jnp.pad(b_forget, (0, LANES - N_HEADS)).reshape(1, LANES), tl["proj_rows"])
        cum = _cumsum_rows(logf.reshape(bsz * N_HEADS, s), tl["cumsum_chunk"]).reshape(bsz, N_HEADS, s)
        y_fox = _fox_attention(qkva, cum, tl["attn_block"])
        y_sb = _sb_attention(qkvb, tl["sb_block"], tl["sb_heads"])
        x = _outproj(x, mod, y_fox, y_sb, wg, w_fox_proj.astype(BF16), w_sb_proj.astype(BF16),
                     w_o.astype(BF16), ln1_g.reshape(1, d), ln1_b.reshape(1, d), tl["proj_rows"])
        x = _ffn(x, mod, w_up.astype(BF16), conv_w, conv_b.reshape(1, 2 * d_ff), w_down.astype(BF16),
                 ln2_g.reshape(1, d), ln2_b.reshape(1, d), tl["ffn_rows"], d_ff // 2)
    return x
```

```python
import functools

import jax
import jax.numpy as jnp
from jax import lax
from jax.experimental import pallas as pl
from jax.experimental.pallas import tpu as pltpu

F32 = jnp.float32
BF16 = jnp.bfloat16

HEAD_DIM = 64
N_HEADS = 8
ATTN_WIDTH = N_HEADS * HEAD_DIM
N_MOD = 6
CONV_WIDTH = 3
LN_EPS = 1e-5
DEPTH = 1
DEEPNORM_ALPHA = (2.0 * DEPTH) ** 0.25
ATTN_SCALE = HEAD_DIM ** -0.5

LANES = 128
SUBLANES = 8
HEADS_PER_VREG_ROW = LANES // HEAD_DIM
VMEM_LIMIT = 56 * 1024 * 1024

EXP_ZERO_BELOW = -110.0


def _cparams(n_axes):
    return pltpu.CompilerParams(
        dimension_semantics=("arbitrary",) * n_axes, vmem_limit_bytes=VMEM_LIMIT)


def _resident(shape):
    nd = len(shape)
    return pl.BlockSpec(shape, lambda *_: (0,) * nd, pipeline_mode=pl.Buffered(1))


def _log_sigmoid(x):
    return jnp.minimum(x, 0.0) - jnp.log1p(jnp.exp(-jnp.abs(x)))


def _layer_norm(y, g, b):
    mu = jnp.mean(y, axis=-1, keepdims=True)
    d = y - mu
    var = jnp.mean(d * d, axis=-1, keepdims=True)
    return d * lax.rsqrt(var + LN_EPS) * g + b


def _ada_kernel(c_ref, w_ref, b_ref, o_ref):
    o_ref[...] = jnp.dot(c_ref[...], w_ref[...], preferred_element_type=F32,
                         precision=lax.Precision.HIGHEST) + b_ref[...]


def _ada(c, w_ada, b_ada):
    bsz, d = c.shape
    n = w_ada.shape[1]
    tn = n // 4
    return pl.pallas_call(
        _ada_kernel,
        grid=(n // tn,),
        in_specs=[pl.BlockSpec((bsz, d), lambda j: (0, 0)),
                  pl.BlockSpec((d, tn), lambda j: (0, j)),
                  pl.BlockSpec((1, tn), lambda j: (0, j))],
        out_specs=pl.BlockSpec((bsz, tn), lambda j: (0, j)),
        out_shape=jax.ShapeDtypeStruct((bsz, n), F32),
        compiler_params=_cparams(1),
        name="adaln_mod",
    )(c, w_ada, b_ada.reshape(1, n))


def _inproj_kernel(x_ref, mod_ref, wa_ref, wb_ref, wf_ref, bf_ref, qkva_ref, qkvb_ref, logf_ref):
    x = x_ref[0]
    sh = mod_ref[0, 0:1, :]
    sc = mod_ref[0, 1:2, :]
    u = (x * (1.0 + sc) + sh).astype(BF16)
    qkva_ref[0] = jnp.dot(u, wa_ref[...], preferred_element_type=F32).astype(BF16)
    qkvb_ref[0] = jnp.dot(u, wb_ref[...], preferred_element_type=F32).astype(BF16)
    f = jnp.dot(u, wf_ref[...], preferred_element_type=F32)
    logf = _log_sigmoid(f + bf_ref[...])
    logf_ref[0] = jnp.transpose(logf)[0:N_HEADS, :]


def _inproj(x, mod, wa, wb, wf, bfg, tm):
    bsz, s, d = x.shape
    na = wa.shape[1]
    return pl.pallas_call(
        _inproj_kernel,
        grid=(bsz, s // tm),
        in_specs=[pl.BlockSpec((1, tm, d), lambda b, i: (b, i, 0)),
                  pl.BlockSpec((1, N_MOD, d), lambda b, i: (b, 0, 0)),
                  _resident(wa.shape), _resident(wb.shape), _resident(wf.shape),
                  _resident(bfg.shape)],
        out_specs=[pl.BlockSpec((1, tm, na), lambda b, i: (b, i, 0)),
                   pl.BlockSpec((1, tm, na), lambda b, i: (b, i, 0)),
                   pl.BlockSpec((1, N_HEADS, tm), lambda b, i: (b, 0, i))],
        out_shape=[jax.ShapeDtypeStruct((bsz, s, na), BF16),
                   jax.ShapeDtypeStruct((bsz, s, na), BF16),
                   jax.ShapeDtypeStruct((bsz, N_HEADS, s), F32)],
        compiler_params=_cparams(2),
        name="in_proj",
    )(x, mod, wa, wb, wf, bfg)


def _split3(x):
    hi = x.astype(BF16)
    r = x - hi.astype(F32)
    mid = r.astype(BF16)
    lo = (r - mid.astype(F32)).astype(BF16)
    return hi, mid, lo


def _cumsum_kernel(x_ref, tri_ref, o_ref, *, chunk):
    rows, s = x_ref.shape
    carry = jnp.zeros((rows, 1), F32)
    for c in range(s // chunk):
        xc = x_ref[:, c * chunk:(c + 1) * chunk]
        cat = jnp.concatenate(_split3(xc), axis=1)
        out = jnp.dot(cat, tri_ref[...], preferred_element_type=F32) + carry
        o_ref[:, c * chunk:(c + 1) * chunk] = out
        carry = out[:, chunk - 1:chunk]


def _cumsum_rows(x, chunk):
    rows, s = x.shape
    j = lax.broadcasted_iota(jnp.int32, (3 * chunk, chunk), 0) % chunk
    t = lax.broadcasted_iota(jnp.int32, (3 * chunk, chunk), 1)
    tri = (j <= t).astype(BF16)
    return pl.pallas_call(
        functools.partial(_cumsum_kernel, chunk=chunk),
        grid=(1,),
        in_specs=[pl.BlockSpec((rows, s), lambda i: (0, 0)),
                  pl.BlockSpec(tri.shape, lambda i: (0, 0))],
        out_specs=pl.BlockSpec((rows, s), lambda i: (0, 0)),
        out_shape=jax.ShapeDtypeStruct((rows, s), F32),
        compiler_params=_cparams(1),
        name="forget_cumsum",
    )(x, tri)


N_DECAY_TERMS = 3


def _foxprep_kernel(qkv_ref, cum_ref, qx_ref, kx_ref, vt_ref):
    tm = qkv_ref.shape[1]
    lane = lax.broadcasted_iota(jnp.int32, (tm, LANES), 1)
    terms = _split3(-cum_ref[0])
    pad = jnp.zeros((LANES - N_DECAY_TERMS * N_HEADS, tm), F32)
    et = jnp.transpose(jnp.concatenate([x.astype(F32) for x in terms] + [pad], axis=0))
    n_pairs = N_HEADS // HEADS_PER_VREG_ROW
    for p in range(n_pairs):
        qp = qkv_ref[0, :, p * LANES:(p + 1) * LANES].astype(F32)
        kp = qkv_ref[0, :, (n_pairs + p) * LANES:(n_pairs + p + 1) * LANES].astype(F32)
        vp = qkv_ref[0, :, (2 * n_pairs + p) * LANES:(2 * n_pairs + p + 1) * LANES].astype(F32)
        vt = jnp.transpose(vp).astype(BF16)
        for r in range(HEADS_PER_VREG_ROW):
            h = HEADS_PER_VREG_ROW * p + r
            own = (lane >= r * HEAD_DIM) & (lane < (r + 1) * HEAD_DIM)
            spare0 = (1 - r) * HEAD_DIM
            qx = jnp.where(own, qp, 0.0)
            kx = jnp.where(own, kp, 0.0)
            for i in range(N_DECAY_TERMS):
                at = lane == spare0 + i
                qx = jnp.where(at, 1.0, qx)
                kx = jnp.where(at, et[:, N_HEADS * i + h:N_HEADS * i + h + 1], kx)
            qx_ref[0, h] = qx.astype(BF16)
            kx_ref[0, h] = kx.astype(BF16)
            vt_ref[0, h] = vt[r * HEAD_DIM:(r + 1) * HEAD_DIM, :]


def _fox_prep(qkv, cum, tm):
    bsz, s, w3 = qkv.shape
    return pl.pallas_call(
        _foxprep_kernel,
        grid=(bsz, s // tm),
        in_specs=[pl.BlockSpec((1, tm, w3), lambda b, i: (b, i, 0)),
                  pl.BlockSpec((1, N_HEADS, tm), lambda b, i: (b, 0, i))],
        out_specs=[pl.BlockSpec((1, N_HEADS, tm, LANES), lambda b, i: (b, 0, i, 0)),
                   pl.BlockSpec((1, N_HEADS, tm, LANES), lambda b, i: (b, 0, i, 0)),
                   pl.BlockSpec((1, N_HEADS, HEAD_DIM, tm), lambda b, i: (b, 0, 0, i))],
        out_shape=[jax.ShapeDtypeStruct((bsz, N_HEADS, s, LANES), BF16),
                   jax.ShapeDtypeStruct((bsz, N_HEADS, s, LANES), BF16),
                   jax.ShapeDtypeStruct((bsz, N_HEADS, HEAD_DIM, s), BF16)],
        compiler_params=_cparams(2),
        name="fox_prep",
    )(qkv, cum)


def _fox_kernel(qx_ref, kx_ref, vt_ref, o_ref, m_ref, l_ref, acc_ref, st_ref, *, t, heads):
    i = pl.program_id(2)
    m_ref[...] = jnp.full(m_ref.shape, -jnp.inf, F32)
    l_ref[...] = jnp.zeros(l_ref.shape, F32)
    acc_ref[...] = jnp.zeros(acc_ref.shape, F32)

    def scores(h, j):
        col0 = pl.multiple_of(j * t, t)
        kh = kx_ref[0, h, pl.ds(col0, t), :]
        return lax.dot_general(kh, qx_ref[0, h], (((1,), (1,)), ((), ())),
                               preferred_element_type=F32)

    def softmax_pv(h, j, diagonal, st_next=None):
        st = st_ref[h]
        if diagonal:
            key = lax.broadcasted_iota(jnp.int32, (t, t), 0)
            qry = lax.broadcasted_iota(jnp.int32, (t, t), 1)
            st = jnp.where(key <= qry, st, -jnp.inf)
        m_prev = m_ref[h]
        m_new = jnp.maximum(m_prev, jnp.max(st, axis=0, keepdims=True))
        alpha = jnp.exp(m_prev - m_new)
        pe = jnp.exp(st - m_new)
        l_ref[h] = alpha * l_ref[h] + jnp.sum(pe, axis=0, keepdims=True)
        col0 = pl.multiple_of(j * t, t)
        vth = vt_ref[0, h, :, pl.ds(col0, t)]
        acc_ref[h] = alpha * acc_ref[h] + jnp.dot(vth, pe.astype(BF16), preferred_element_type=F32)
        m_ref[h] = m_new
        if st_next is not None:
            st_ref[h] = st_next

    for h in range(heads):
        st_ref[h] = scores(h, 0)

    def body(j, carry):
        for h in range(heads):
            softmax_pv(h, j, False, st_next=scores(h, j + 1))
        return carry

    lax.fori_loop(0, i, body, 0)
    for h in range(heads):
        softmax_pv(h, i, True)

    for p in range(heads // HEADS_PER_VREG_ROW):
        pair = [acc_ref[HEADS_PER_VREG_ROW * p + r] / l_ref[HEADS_PER_VREG_ROW * p + r]
                for r in range(HEADS_PER_VREG_ROW)]
        o_ref[0, :, p * LANES:(p + 1) * LANES] = jnp.transpose(
            jnp.concatenate(pair, axis=0)).astype(o_ref.dtype)


def _fox_attention(qx, kx, vt, t, heads):
    bsz, _, s, _ = qx.shape
    n_groups = N_HEADS // heads
    whole_seq = dict(pipeline_mode=pl.Buffered(1))
    return pl.pallas_call(
        functools.partial(_fox_kernel, t=t, heads=heads),
        grid=(bsz, n_groups, s // t),
        in_specs=[pl.BlockSpec((1, heads, t, LANES), lambda b, g, i: (b, g, i, 0)),
                  pl.BlockSpec((1, heads, s, LANES), lambda b, g, i: (b, g, 0, 0), **whole_seq),
                  pl.BlockSpec((1, heads, HEAD_DIM, s), lambda b, g, i: (b, g, 0, 0), **whole_seq)],
        out_specs=pl.BlockSpec((1, t, heads * HEAD_DIM), lambda b, g, i: (b, i, g)),
        out_shape=jax.ShapeDtypeStruct((bsz, s, ATTN_WIDTH), BF16),
        scratch_shapes=[pltpu.VMEM((heads, 1, t), F32),
                        pltpu.VMEM((heads, 1, t), F32),
                        pltpu.VMEM((heads, HEAD_DIM, t), F32),
                        pltpu.VMEM((heads, t, t), F32)],
        compiler_params=_cparams(3),
        name="fox_attention",
    )(qx, kx, vt)


def _sb_kernel(q_ref, k_ref, v_ref, u_ref, o_ref, acc_ref, carry_ref, *, t, heads):
    i = pl.program_id(2)
    n_pairs = heads // HEADS_PER_VREG_ROW
    lane = lax.broadcasted_iota(jnp.int32, (t, LANES), 1)
    qs = []
    for p in range(n_pairs):
        q = q_ref[0, :, p * LANES:(p + 1) * LANES]
        zero = jnp.zeros_like(q)
        qs.append((jnp.where(lane < HEAD_DIM, q, zero), jnp.where(lane >= HEAD_DIM, q, zero)))

    acc_ref[...] = jnp.zeros(acc_ref.shape, F32)
    carry_ref[...] = jnp.zeros(carry_ref.shape, F32)

    def step(j, diagonal):
        col0 = pl.multiple_of(j * t, t)
        kb = k_ref[0, pl.ds(col0, t), :]
        vb = v_ref[0, pl.ds(col0, t), :]
        if diagonal:
            strict = (lax.broadcasted_iota(jnp.int32, (t, t), 1)
                      < lax.broadcasted_iota(jnp.int32, (t, t), 0))
        zs = []
        for h in range(heads):
            p, r = divmod(h, HEADS_PER_VREG_ROW)
            ks = kb[:, p * LANES:(p + 1) * LANES]
            zs.append(lax.dot_general(qs[p][r], ks, (((1,), (1,)), ((), ())), preferred_element_type=F32))
        reds = []
        for h in range(heads):
            z = zs[h]
            sp = jnp.maximum(z, 0.0) + jnp.log1p(jnp.exp(-jnp.abs(z)))
            if diagonal:
                sp = jnp.where(strict, sp, 0.0)
            hi = sp.astype(BF16)
            lo = (sp - hi.astype(F32)).astype(BF16)
            reds.append(jnp.dot(jnp.concatenate([hi, lo], axis=1), u_ref[...], preferred_element_type=F32))
        worst = None
        for h in range(heads):
            p = h // HEADS_PER_VREG_ROW
            vs = vb[:, p * LANES:(p + 1) * LANES]
            carry = carry_ref[h]
            a = jnp.exp(zs[h] + reds[h][:, :t] + carry)
            if diagonal:
                a = jnp.where(strict, a, 0.0)
            acc_ref[h] += jnp.dot(a.astype(BF16), vs, preferred_element_type=F32)
            carry = carry + reds[h][:, t:]
            carry_ref[h] = carry
            worst = carry if worst is None else jnp.maximum(worst, carry)
        return jnp.max(worst)

    worst0 = step(i, True)

    def cond(state):
        j, worst = state
        return jnp.logical_and(j >= 0, worst >= EXP_ZERO_BELOW)

    def body(state):
        j, _ = state
        return j - 1, step(j, False)

    lax.while_loop(cond, body, (i - 1, worst0))

    for p in range(n_pairs):
        o_ref[0, :, p * LANES:(p + 1) * LANES] = jnp.where(
            lane < HEAD_DIM, acc_ref[HEADS_PER_VREG_ROW * p], acc_ref[HEADS_PER_VREG_ROW * p + 1]
        ).astype(o_ref.dtype)


def _sb_attention(qkv, t, heads):
    bsz, s, _ = qkv.shape
    n_groups = N_HEADS // heads
    w = heads * HEAD_DIM
    j = lax.broadcasted_iota(jnp.int32, (2 * t, 2 * t), 0) % t
    c = lax.broadcasted_iota(jnp.int32, (2 * t, 2 * t), 1)
    u = -jnp.logical_or(c >= t, j >= c).astype(BF16)
    return pl.pallas_call(
        functools.partial(_sb_kernel, t=t, heads=heads),
        grid=(bsz, n_groups, s // t),
        in_specs=[pl.BlockSpec((1, t, w), lambda b, g, i: (b, i, g)),
                  pl.BlockSpec((1, s, w), lambda b, g, i: (b, 0, n_groups + g)),
                  pl.BlockSpec((1, s, w), lambda b, g, i: (b, 0, 2 * n_groups + g)),
                  pl.BlockSpec(u.shape, lambda b, g, i: (0, 0))],
        out_specs=pl.BlockSpec((1, t, w), lambda b, g, i: (b, i, g)),
        out_shape=jax.ShapeDtypeStruct((bsz, s, ATTN_WIDTH), BF16),
        scratch_shapes=[pltpu.VMEM((heads, t, LANES), F32),
                        pltpu.VMEM((heads, t, LANES), F32)],
        compiler_params=_cparams(3),
        name="stickbreak_attention",
    )(qkv, qkv, qkv, u)


def _outproj_kernel(x_ref, mod_ref, yf_ref, ys_ref, wg_ref, wfp_ref, wsp_ref, wo_ref, g_ref, b_ref, o_ref):
    x = x_ref[0]
    d = x.shape[-1]
    sh = mod_ref[0, 0:1, :]
    sc = mod_ref[0, 1:2, :]
    gt = mod_ref[0, 2:3, :]
    u = (x * (1.0 + sc) + sh).astype(BF16)
    gates = jnp.dot(u, wg_ref[...], preferred_element_type=F32)
    pa = jnp.dot(yf_ref[0], wfp_ref[...], preferred_element_type=F32)
    pb = jnp.dot(ys_ref[0], wsp_ref[...], preferred_element_type=F32)
    merged = jax.nn.sigmoid(gates[:, :d]) * pa + jax.nn.sigmoid(gates[:, d:]) * pb
    attn = jnp.dot(merged.astype(BF16), wo_ref[...], preferred_element_type=F32)
    y = DEEPNORM_ALPHA * x + (1.0 + gt) * attn
    o_ref[0] = _layer_norm(y, g_ref[...], b_ref[...])


def _outproj(x, mod, yf, ys, wg, wfp, wsp, wo, g, b, tm):
    bsz, s, d = x.shape
    aw = yf.shape[-1]
    return pl.pallas_call(
        _outproj_kernel,
        grid=(bsz, s // tm),
        in_specs=[pl.BlockSpec((1, tm, d), lambda b_, i: (b_, i, 0)),
                  pl.BlockSpec((1, N_MOD, d), lambda b_, i: (b_, 0, 0)),
                  pl.BlockSpec((1, tm, aw), lambda b_, i: (b_, i, 0)),
                  pl.BlockSpec((1, tm, aw), lambda b_, i: (b_, i, 0)),
                  _resident(wg.shape), _resident(wfp.shape), _resident(wsp.shape), _resident(wo.shape),
                  _resident(g.shape), _resident(b.shape)],
        out_specs=pl.BlockSpec((1, tm, d), lambda b_, i: (b_, i, 0)),
        out_shape=jax.ShapeDtypeStruct((bsz, s, d), F32),
        compiler_params=_cparams(2),
        name="merge_out_proj_ln",
    )(x, mod, yf, ys, wg, wfp, wsp, wo, g, b)


def _ffn_kernel(x_ref, mod_ref, wup_ref, cw_ref, cb_ref, wdn_ref, g_ref, b_ref, o_ref,
                tail_ref, hbuf_ref, *, tm, fc):
    i = pl.program_id(1)
    d_ff = wdn_ref.shape[0]
    halo = SUBLANES

    @pl.when(i == 0)
    def _():
        tail_ref[...] = jnp.zeros(tail_ref.shape, F32)

    x = x_ref[0]
    sh = mod_ref[0, 3:4, :]
    sc = mod_ref[0, 4:5, :]
    gt = mod_ref[0, 5:6, :]
    u = (x * (1.0 + sc) + sh).astype(BF16)

    def conv_part(col0):
        h = jnp.dot(u, wup_ref[:, col0:col0 + fc], preferred_element_type=F32)
        hbuf_ref[0:halo, :] = tail_ref[:, col0:col0 + fc]
        hbuf_ref[halo:halo + tm, :] = h
        tail_ref[:, col0:col0 + fc] = h[tm - halo:tm, :]
        h1 = hbuf_ref[halo - 1:halo - 1 + tm, :]
        h2 = hbuf_ref[halo - 2:halo - 2 + tm, :]
        w = cw_ref[:, col0:col0 + fc]
        out = cb_ref[:, col0:col0 + fc] + h2 * w[0:1, :]
        out = out + h1 * w[1:2, :]
        return out + h * w[2:3, :]

    acc = jnp.zeros(x.shape, F32)
    for c in range(d_ff // fc):
        hg = conv_part(c * fc)
        hv = conv_part(d_ff + c * fc)
        act = (hg * jax.nn.sigmoid(hg)) * hv
        acc = acc + jnp.dot(act.astype(BF16), wdn_ref[c * fc:(c + 1) * fc, :], preferred_element_type=F32)

    y = DEEPNORM_ALPHA * x + (1.0 + gt) * acc
    o_ref[0] = _layer_norm(y, g_ref[...], b_ref[...])


def _ffn(x, mod, wup, cw, cb, wdn, g, b, tm, fc):
    bsz, s, d = x.shape
    f2 = wup.shape[1]
    return pl.pallas_call(
        functools.partial(_ffn_kernel, tm=tm, fc=fc),
        grid=(bsz, s // tm),
        in_specs=[pl.BlockSpec((1, tm, d), lambda b_, i: (b_, i, 0)),
                  pl.BlockSpec((1, N_MOD, d), lambda b_, i: (b_, 0, 0)),
                  _resident(wup.shape), _resident(cw.shape), _resident(cb.shape), _resident(wdn.shape),
                  _resident(g.shape), _resident(b.shape)],
        out_specs=pl.BlockSpec((1, tm, d), lambda b_, i: (b_, i, 0)),
        out_shape=jax.ShapeDtypeStruct((bsz, s, d), F32),
        scratch_shapes=[pltpu.VMEM((SUBLANES, f2), F32),
                        pltpu.VMEM((SUBLANES + tm, fc), F32)],
        compiler_params=_cparams(2),
        name="conv_ffn_ln",
    )(x, mod, wup, cw, cb, wdn, g, b)


def _tiles(s):
    return dict(
        proj_rows=min(512, s),
        attn_block=min(256, s),
        attn_heads=N_HEADS,
        sb_block=LANES,
        sb_heads=N_HEADS,
        cumsum_chunk=min(512, s),
        ffn_rows=min(512, s),
    )


def kernel(x, c, w_ada, b_ada, w_in, b_forget, w_fox_proj, w_sb_proj, w_o, ln1_g, ln1_b,
           w_up, conv_w, conv_b, w_down, ln2_g, ln2_b):
    bsz, s, d = x.shape
    d_ff = w_down.shape[0]
    tl = _tiles(s)
    aw = ATTN_WIDTH

    qscale = jnp.concatenate([jnp.full((aw,), ATTN_SCALE, F32), jnp.ones((2 * aw,), F32)])
    wa = (w_in[:, 0:3 * aw] * qscale).astype(BF16)
    wf = jnp.pad(w_in[:, 3 * aw:3 * aw + N_HEADS], ((0, 0), (0, LANES - N_HEADS))).astype(BF16)
    b0 = 3 * aw + N_HEADS
    wb = (w_in[:, b0:b0 + 3 * aw] * qscale).astype(BF16)
    wg = w_in[:, b0 + 3 * aw:].astype(BF16)

    for _ in range(DEPTH):
        mod = _ada(c, w_ada, b_ada).reshape(bsz, N_MOD, d)
        qkva, qkvb, logf = _inproj(x, mod, wa, wb, wf, jnp.pad(b_forget, (0, LANES - N_HEADS)).reshape(1, LANES), tl["proj_rows"])
        cum = _cumsum_rows(logf.reshape(bsz * N_HEADS, s), tl["cumsum_chunk"]).reshape(bsz, N_HEADS, s)
        qx, kx, vt = _fox_prep(qkva, cum, tl["proj_rows"])
        y_fox = _fox_attention(qx, kx, vt, tl["attn_block"], tl["attn_heads"])
        y_sb = _sb_attention(qkvb, tl["sb_block"], tl["sb_heads"])
        x = _outproj(x, mod, y_fox, y_sb, wg, w_fox_proj.astype(BF16), w_sb_proj.astype(BF16),
                     w_o.astype(BF16), ln1_g.reshape(1, d), ln1_b.reshape(1, d), tl["proj_rows"])
        x = _ffn(x, mod, w_up.astype(BF16), conv_w, conv_b.reshape(1, 2 * d_ff), w_down.astype(BF16),
                 ln2_g.reshape(1, d), ln2_b.reshape(1, d), tl["ffn_rows"], d_ff // 2)
    return x
```

```python
import functools

import jax
import jax.numpy as jnp
from jax import lax
from jax.experimental import pallas as pl
from jax.experimental.pallas import tpu as pltpu

F32 = jnp.float32
BF16 = jnp.bfloat16

HEAD_DIM = 64
N_HEADS = 8
ATTN_WIDTH = N_HEADS * HEAD_DIM
N_MOD = 6
CONV_WIDTH = 3
LN_EPS = 1e-5
DEPTH = 1
DEEPNORM_ALPHA = (2.0 * DEPTH) ** 0.25
ATTN_SCALE = HEAD_DIM ** -0.5

LANES = 128
SUBLANES = 8
HEADS_PER_VREG_ROW = LANES // HEAD_DIM
VMEM_LIMIT = 56 * 1024 * 1024

EXP_ZERO_BELOW = -110.0


def _cparams(n_axes):
    return pltpu.CompilerParams(
        dimension_semantics=("arbitrary",) * n_axes, vmem_limit_bytes=VMEM_LIMIT)


def _resident(shape):
    nd = len(shape)
    return pl.BlockSpec(shape, lambda *_: (0,) * nd, pipeline_mode=pl.Buffered(1))


def _log_sigmoid(x):
    return jnp.minimum(x, 0.0) - jnp.log1p(jnp.exp(-jnp.abs(x)))


def _layer_norm(y, g, b):
    mu = jnp.mean(y, axis=-1, keepdims=True)
    d = y - mu
    var = jnp.mean(d * d, axis=-1, keepdims=True)
    return d * lax.rsqrt(var + LN_EPS) * g + b


def _ada_kernel(c_ref, w_ref, b_ref, o_ref):
    o_ref[...] = jnp.dot(c_ref[...], w_ref[...], preferred_element_type=F32,
                         precision=lax.Precision.HIGHEST) + b_ref[...]


def _ada(c, w_ada, b_ada):
    bsz, d = c.shape
    n = w_ada.shape[1]
    tn = n // 4
    return pl.pallas_call(
        _ada_kernel,
        grid=(n // tn,),
        in_specs=[pl.BlockSpec((bsz, d), lambda j: (0, 0)),
                  pl.BlockSpec((d, tn), lambda j: (0, j)),
                  pl.BlockSpec((1, tn), lambda j: (0, j))],
        out_specs=pl.BlockSpec((bsz, tn), lambda j: (0, j)),
        out_shape=jax.ShapeDtypeStruct((bsz, n), F32),
        compiler_params=_cparams(1),
        name="adaln_mod",
    )(c, w_ada, b_ada.reshape(1, n))


def _inproj_kernel(x_ref, mod_ref, wa_ref, wb_ref, wf_ref, bf_ref, qkva_ref, qkvb_ref, logf_ref):
    x = x_ref[0]
    sh = mod_ref[0, 0:1, :]
    sc = mod_ref[0, 1:2, :]
    u = (x * (1.0 + sc) + sh).astype(BF16)
    qkva_ref[0] = jnp.dot(u, wa_ref[...], preferred_element_type=F32).astype(BF16)
    qkvb_ref[0] = jnp.dot(u, wb_ref[...], preferred_element_type=F32).astype(BF16)
    f = jnp.dot(u, wf_ref[...], preferred_element_type=F32)
    logf = _log_sigmoid(f + bf_ref[...])
    logf_ref[0] = jnp.transpose(logf)[0:N_HEADS, :]


def _inproj(x, mod, wa, wb, wf, bfg, tm):
    bsz, s, d = x.shape
    na = wa.shape[1]
    return pl.pallas_call(
        _inproj_kernel,
        grid=(bsz, s // tm),
        in_specs=[pl.BlockSpec((1, tm, d), lambda b, i: (b, i, 0)),
                  pl.BlockSpec((1, N_MOD, d), lambda b, i: (b, 0, 0)),
                  _resident(wa.shape), _resident(wb.shape), _resident(wf.shape),
                  _resident(bfg.shape)],
        out_specs=[pl.BlockSpec((1, tm, na), lambda b, i: (b, i, 0)),
                   pl.BlockSpec((1, tm, na), lambda b, i: (b, i, 0)),
                   pl.BlockSpec((1, N_HEADS, tm), lambda b, i: (b, 0, i))],
        out_shape=[jax.ShapeDtypeStruct((bsz, s, na), BF16),
                   jax.ShapeDtypeStruct((bsz, s, na), BF16),
                   jax.ShapeDtypeStruct((bsz, N_HEADS, s), F32)],
        compiler_params=_cparams(2),
        name="in_proj",
    )(x, mod, wa, wb, wf, bfg)


def _split3(x):
    hi = x.astype(BF16)
    r = x - hi.astype(F32)
    mid = r.astype(BF16)
    lo = (r - mid.astype(F32)).astype(BF16)
    return hi, mid, lo


def _cumsum_kernel(x_ref, tri_ref, o_ref, *, chunk):
    rows, s = x_ref.shape
    carry = jnp.zeros((rows, 1), F32)
    for c in range(s // chunk):
        xc = x_ref[:, c * chunk:(c + 1) * chunk]
        cat = jnp.concatenate(_split3(xc), axis=1)
        out = jnp.dot(cat, tri_ref[...], preferred_element_type=F32) + carry
        o_ref[:, c * chunk:(c + 1) * chunk] = out
        carry = out[:, chunk - 1:chunk]


def _cumsum_rows(x, chunk):
    rows, s = x.shape
    j = lax.broadcasted_iota(jnp.int32, (3 * chunk, chunk), 0) % chunk
    t = lax.broadcasted_iota(jnp.int32, (3 * chunk, chunk), 1)
    tri = (j <= t).astype(BF16)
    return pl.pallas_call(
        functools.partial(_cumsum_kernel, chunk=chunk),
        grid=(1,),
        in_specs=[pl.BlockSpec((rows, s), lambda i: (0, 0)),
                  pl.BlockSpec(tri.shape, lambda i: (0, 0))],
        out_specs=pl.BlockSpec((rows, s), lambda i: (0, 0)),
        out_shape=jax.ShapeDtypeStruct((rows, s), F32),
        compiler_params=_cparams(1),
        name="forget_cumsum",
    )(x, tri)


N_DECAY_TERMS = 3


def _foxprep_kernel(qkv_ref, cum_ref, qx_ref, kx_ref, vt_ref, qn_ref, kn_ref):
    tm = qkv_ref.shape[1]
    lane = lax.broadcasted_iota(jnp.int32, (tm, LANES), 1)
    terms = _split3(-cum_ref[0])
    pad = jnp.zeros((LANES - N_DECAY_TERMS * N_HEADS, tm), F32)
    et = jnp.transpose(jnp.concatenate([x.astype(F32) for x in terms] + [pad], axis=0))
    n_pairs = N_HEADS // HEADS_PER_VREG_ROW
    qns, kns = [], []

    def max_sq_norm(x):
        rows = jnp.sum(x * x, axis=1, keepdims=True)
        return jnp.broadcast_to(jnp.max(rows, axis=0, keepdims=True), (1, LANES))

    for p in range(n_pairs):
        qp = qkv_ref[0, :, p * LANES:(p + 1) * LANES].astype(F32)
        kp = qkv_ref[0, :, (n_pairs + p) * LANES:(n_pairs + p + 1) * LANES].astype(F32)
        vp = qkv_ref[0, :, (2 * n_pairs + p) * LANES:(2 * n_pairs + p + 1) * LANES].astype(F32)
        vt = jnp.transpose(vp).astype(BF16)
        for r in range(HEADS_PER_VREG_ROW):
            h = HEADS_PER_VREG_ROW * p + r
            own = (lane >= r * HEAD_DIM) & (lane < (r + 1) * HEAD_DIM)
            spare0 = (1 - r) * HEAD_DIM
            qx = jnp.where(own, qp, 0.0)
            kx = jnp.where(own, kp, 0.0)
            qns.append(max_sq_norm(qx))
            kns.append(max_sq_norm(kx))
            for i in range(N_DECAY_TERMS):
                at = lane == spare0 + i
                qx = jnp.where(at, 1.0, qx)
                kx = jnp.where(at, et[:, N_HEADS * i + h:N_HEADS * i + h + 1], kx)
            qx_ref[0, h] = qx.astype(BF16)
            kx_ref[0, h] = kx.astype(BF16)
            vt_ref[0, h] = vt[r * HEAD_DIM:(r + 1) * HEAD_DIM, :]
    qn_ref[0, 0] = jnp.concatenate(qns, axis=0)
    kn_ref[0, 0] = jnp.concatenate(kns, axis=0)


def _fox_prep(qkv, cum, tm):
    bsz, s, w3 = qkv.shape
    nt = s // tm
    return pl.pallas_call(
        _foxprep_kernel,
        grid=(bsz, nt),
        in_specs=[pl.BlockSpec((1, tm, w3), lambda b, i: (b, i, 0)),
                  pl.BlockSpec((1, N_HEADS, tm), lambda b, i: (b, 0, i))],
        out_specs=[pl.BlockSpec((1, N_HEADS, tm, LANES), lambda b, i: (b, 0, i, 0)),
                   pl.BlockSpec((1, N_HEADS, tm, LANES), lambda b, i: (b, 0, i, 0)),
                   pl.BlockSpec((1, N_HEADS, HEAD_DIM, tm), lambda b, i: (b, 0, 0, i)),
                   pl.BlockSpec((1, 1, N_HEADS, LANES), lambda b, i: (b, i, 0, 0)),
                   pl.BlockSpec((1, 1, N_HEADS, LANES), lambda b, i: (b, i, 0, 0))],
        out_shape=[jax.ShapeDtypeStruct((bsz, N_HEADS, s, LANES), BF16),
                   jax.ShapeDtypeStruct((bsz, N_HEADS, s, LANES), BF16),
                   jax.ShapeDtypeStruct((bsz, N_HEADS, HEAD_DIM, s), BF16),
                   jax.ShapeDtypeStruct((bsz, nt, N_HEADS, LANES), F32),
                   jax.ShapeDtypeStruct((bsz, nt, N_HEADS, LANES), F32)],
        compiler_params=_cparams(2),
        name="fox_prep",
    )(qkv, cum)


def _fox_kernel(clast_ref, qkmax_ref, qx_ref, kx_ref, vt_ref, o_ref,
                m_ref, l_ref, acc_ref, st_ref, *, tq, tk):
    b = pl.program_id(0)
    h = pl.program_id(1)
    i = pl.program_id(2)
    n_diag = tq // tk
    n_qblk = pl.num_programs(2)
    bh = b * pl.num_programs(1) + h
    m_ref[...] = jnp.full(m_ref.shape, -jnp.inf, F32)
    l_ref[...] = jnp.zeros(l_ref.shape, F32)
    acc_ref[...] = jnp.zeros(acc_ref.shape, F32)

    def scores(jb, c0):
        row0 = pl.multiple_of(jb * tk, tk)
        return lax.dot_general(kx_ref[0, 0, pl.ds(row0, tk), :], qx_ref[0, 0, c0:, :],
                               (((1,), (1,)), ((), ())), preferred_element_type=F32)

    def softmax_pv(jb, c0, diagonal):
        st = st_ref[:, c0:]
        if diagonal:
            key = lax.broadcasted_iota(jnp.int32, st.shape, 0)
            qry = lax.broadcasted_iota(jnp.int32, st.shape, 1)
            st = jnp.where(key <= qry, st, -jnp.inf)
        m_prev = m_ref[:, c0:]
        m_new = jnp.maximum(m_prev, jnp.max(st, axis=0, keepdims=True))
        alpha = jnp.exp(m_prev - m_new)
        pe = jnp.exp(st - m_new)
        l_ref[:, c0:] = alpha * l_ref[:, c0:] + jnp.sum(pe, axis=0, keepdims=True)
        row0 = pl.multiple_of(jb * tk, tk)
        vblk = vt_ref[0, 0, :, pl.ds(row0, tk)]
        acc_ref[:, c0:] = alpha * acc_ref[:, c0:] + jnp.dot(vblk, pe.astype(BF16),
                                                            preferred_element_type=F32)
        m_ref[:, c0:] = m_new

    first = i * n_diag
    st_ref[:, (n_diag - 1) * tk:] = scores(first + n_diag - 1, (n_diag - 1) * tk)
    for d in range(n_diag - 1, 0, -1):
        nxt = scores(first + d - 1, (d - 1) * tk)
        softmax_pv(first + d, d * tk, True)
        st_ref[:, (d - 1) * tk:] = nxt
    nxt = scores(jnp.maximum(first - 1, 0), 0)
    softmax_pv(first, 0, True)
    st_ref[...] = nxt

    bound_qk = qkmax_ref[bh * n_qblk + i]
    n_kblk = n_qblk * n_diag

    def cond(state):
        j, m_min = state
        reach = bound_qk - clast_ref[bh * n_kblk + jnp.maximum(j, 0)] - m_min
        return jnp.logical_and(j >= 0, reach >= EXP_ZERO_BELOW)

    def body(state):
        j, _ = state
        m_min = jnp.min(m_ref[...])
        nxt = scores(jnp.maximum(j - 1, 0), 0)
        softmax_pv(j, 0, False)
        st_ref[...] = nxt
        return j - 1, m_min

    lax.while_loop(cond, body, (first - 1, jnp.min(m_ref[...])))
    o_ref[0, 0] = (acc_ref[...] / l_ref[...]).astype(o_ref.dtype)


def _fox_attention(qx, kx, vt, clast, qkmax, tq, tk):
    bsz, nh, s, _ = qx.shape
    grid_spec = pltpu.PrefetchScalarGridSpec(
        num_scalar_prefetch=2,
        grid=(bsz, nh, s // tq),
        in_specs=[pl.BlockSpec((1, 1, tq, LANES), lambda b, h, i, *_: (b, h, i, 0)),
                  pl.BlockSpec((1, 1, s, LANES), lambda b, h, i, *_: (b, h, 0, 0)),
                  pl.BlockSpec((1, 1, HEAD_DIM, s), lambda b, h, i, *_: (b, h, 0, 0))],
        out_specs=pl.BlockSpec((1, 1, HEAD_DIM, tq), lambda b, h, i, *_: (b, h, 0, i)),
        scratch_shapes=[pltpu.VMEM((1, tq), F32),
                        pltpu.VMEM((1, tq), F32),
                        pltpu.VMEM((HEAD_DIM, tq), F32),
                        pltpu.VMEM((tk, tq), F32)])
    return pl.pallas_call(
        functools.partial(_fox_kernel, tq=tq, tk=tk),
        grid_spec=grid_spec,
        out_shape=jax.ShapeDtypeStruct((bsz, nh, HEAD_DIM, s), BF16),
        compiler_params=_cparams(3),
        name="fox_attention",
    )(clast, qkmax, qx, kx, vt)


def _sb_kernel(q_ref, k_ref, v_ref, u_ref, o_ref, acc_ref, carry_ref, *, t, heads):
    i = pl.program_id(2)
    n_pairs = heads // HEADS_PER_VREG_ROW
    lane = lax.broadcasted_iota(jnp.int32, (t, LANES), 1)
    qs = []
    for p in range(n_pairs):
        q = q_ref[0, :, p * LANES:(p + 1) * LANES]
        zero = jnp.zeros_like(q)
        qs.append((jnp.where(lane < HEAD_DIM, q, zero), jnp.where(lane >= HEAD_DIM, q, zero)))

    acc_ref[...] = jnp.zeros(acc_ref.shape, F32)
    carry_ref[...] = jnp.zeros(carry_ref.shape, F32)

    def step(j, diagonal):
        col0 = pl.multiple_of(j * t, t)
        kb = k_ref[0, pl.ds(col0, t), :]
        vb = v_ref[0, pl.ds(col0, t), :]
        if diagonal:
            strict = (lax.broadcasted_iota(jnp.int32, (t, t), 1)
                      < lax.broadcasted_iota(jnp.int32, (t, t), 0))
        zs = []
        for h in range(heads):
            p, r = divmod(h, HEADS_PER_VREG_ROW)
            ks = kb[:, p * LANES:(p + 1) * LANES]
            zs.append(lax.dot_general(qs[p][r], ks, (((1,), (1,)), ((), ())), preferred_element_type=F32))
        reds = []
        for h in range(heads):
            z = zs[h]
            sp = jnp.maximum(z, 0.0) + jnp.log(1.0 + jnp.exp(-jnp.abs(z)))
            if diagonal:
                sp = jnp.where(strict, sp, 0.0)
            hi = sp.astype(BF16)
            lo = (sp - hi.astype(F32)).astype(BF16)
            reds.append(jnp.dot(jnp.concatenate([hi, lo], axis=1), u_ref[...], preferred_element_type=F32))
        worst = None
        for h in range(heads):
            p = h // HEADS_PER_VREG_ROW
            vs = vb[:, p * LANES:(p + 1) * LANES]
            carry = carry_ref[h]
            a = jnp.exp(zs[h] + reds[h][:, :t] + carry)
            if diagonal:
                a = jnp.where(strict, a, 0.0)
            acc_ref[h] += jnp.dot(a.astype(BF16), vs, preferred_element_type=F32)
            carry = carry + reds[h][:, t:]
            carry_ref[h] = carry
            worst = carry if worst is None else jnp.maximum(worst, carry)
        return jnp.max(worst)

    worst0 = step(i, True)

    def cond(state):
        j, worst = state
        return jnp.logical_and(j >= 0, worst >= EXP_ZERO_BELOW)

    def body(state):
        j, _ = state
        return j - 1, step(j, False)

    lax.while_loop(cond, body, (i - 1, worst0))

    for p in range(n_pairs):
        o_ref[0, :, p * LANES:(p + 1) * LANES] = jnp.where(
            lane < HEAD_DIM, acc_ref[HEADS_PER_VREG_ROW * p], acc_ref[HEADS_PER_VREG_ROW * p + 1]
        ).astype(o_ref.dtype)


def _sb_attention(qkv, t, heads):
    bsz, s, _ = qkv.shape
    n_groups = N_HEADS // heads
    w = heads * HEAD_DIM
    j = lax.broadcasted_iota(jnp.int32, (2 * t, 2 * t), 0) % t
    c = lax.broadcasted_iota(jnp.int32, (2 * t, 2 * t), 1)
    u = -jnp.logical_or(c >= t, j >= c).astype(BF16)
    return pl.pallas_call(
        functools.partial(_sb_kernel, t=t, heads=heads),
        grid=(bsz, n_groups, s // t),
        in_specs=[pl.BlockSpec((1, t, w), lambda b, g, i: (b, i, g)),
                  pl.BlockSpec((1, s, w), lambda b, g, i: (b, 0, n_groups + g)),
                  pl.BlockSpec((1, s, w), lambda b, g, i: (b, 0, 2 * n_groups + g)),
                  pl.BlockSpec(u.shape, lambda b, g, i: (0, 0))],
        out_specs=pl.BlockSpec((1, t, w), lambda b, g, i: (b, i, g)),
        out_shape=jax.ShapeDtypeStruct((bsz, s, ATTN_WIDTH), BF16),
        scratch_shapes=[pltpu.VMEM((heads, t, LANES), F32),
                        pltpu.VMEM((heads, t, LANES), F32)],
        compiler_params=_cparams(3),
        name="stickbreak_attention",
    )(qkv, qkv, qkv, u)


def _outproj_kernel(x_ref, mod_ref, yf_ref, ys_ref, wg_ref, wfp_ref, wsp_ref, wo_ref, g_ref, b_ref, o_ref):
    x = x_ref[0]
    d = x.shape[-1]
    sh = mod_ref[0, 0:1, :]
    sc = mod_ref[0, 1:2, :]
    gt = mod_ref[0, 2:3, :]
    u = (x * (1.0 + sc) + sh).astype(BF16)
    gates = jnp.dot(u, wg_ref[...], preferred_element_type=F32)
    pa = lax.dot_general(yf_ref[0], wfp_ref[...], (((0,), (0,)), ((), ())), preferred_element_type=F32)
    pb = jnp.dot(ys_ref[0], wsp_ref[...], preferred_element_type=F32)
    merged = jax.nn.sigmoid(gates[:, :d]) * pa + jax.nn.sigmoid(gates[:, d:]) * pb
    attn = jnp.dot(merged.astype(BF16), wo_ref[...], preferred_element_type=F32)
    y = DEEPNORM_ALPHA * x + (1.0 + gt) * attn
    o_ref[0] = _layer_norm(y, g_ref[...], b_ref[...])


def _outproj(x, mod, yf, ys, wg, wfp, wsp, wo, g, b, tm):
    bsz, s, d = x.shape
    aw = ys.shape[-1]
    return pl.pallas_call(
        _outproj_kernel,
        grid=(bsz, s // tm),
        in_specs=[pl.BlockSpec((1, tm, d), lambda b_, i: (b_, i, 0)),
                  pl.BlockSpec((1, N_MOD, d), lambda b_, i: (b_, 0, 0)),
                  pl.BlockSpec((1, aw, tm), lambda b_, i: (b_, 0, i)),
                  pl.BlockSpec((1, tm, aw), lambda b_, i: (b_, i, 0)),
                  _resident(wg.shape), _resident(wfp.shape), _resident(wsp.shape), _resident(wo.shape),
                  _resident(g.shape), _resident(b.shape)],
        out_specs=pl.BlockSpec((1, tm, d), lambda b_, i: (b_, i, 0)),
        out_shape=jax.ShapeDtypeStruct((bsz, s, d), F32),
        compiler_params=_cparams(2),
        name="merge_out_proj_ln",
    )(x, mod, yf, ys, wg, wfp, wsp, wo, g, b)


def _ffn_kernel(x_ref, mod_ref, wup_ref, cw_ref, cb_ref, wdn_ref, g_ref, b_ref, o_ref,
                tail_ref, hbuf_ref, *, tm, fc):
    i = pl.program_id(1)
    d_ff = wdn_ref.shape[0]
    halo = SUBLANES

    @pl.when(i == 0)
    def _():
        tail_ref[...] = jnp.zeros(tail_ref.shape, F32)

    x = x_ref[0]
    sh = mod_ref[0, 3:4, :]
    sc = mod_ref[0, 4:5, :]
    gt = mod_ref[0, 5:6, :]
    u = (x * (1.0 + sc) + sh).astype(BF16)

    def conv_part(col0):
        h = jnp.dot(u, wup_ref[:, col0:col0 + fc], preferred_element_type=F32)
        hbuf_ref[0:halo, :] = tail_ref[:, col0:col0 + fc]
        hbuf_ref[halo:halo + tm, :] = h
        tail_ref[:, col0:col0 + fc] = h[tm - halo:tm, :]
        h1 = hbuf_ref[halo - 1:halo - 1 + tm, :]
        h2 = hbuf_ref[halo - 2:halo - 2 + tm, :]
        w = cw_ref[:, col0:col0 + fc]
        out = cb_ref[:, col0:col0 + fc] + h2 * w[0:1, :]
        out = out + h1 * w[1:2, :]
        return out + h * w[2:3, :]

    acc = jnp.zeros(x.shape, F32)
    for c in range(d_ff // fc):
        hg = conv_part(c * fc)
        hv = conv_part(d_ff + c * fc)
        act = (hg * jax.nn.sigmoid(hg)) * hv
        acc = acc + jnp.dot(act.astype(BF16), wdn_ref[c * fc:(c + 1) * fc, :], preferred_element_type=F32)

    y = DEEPNORM_ALPHA * x + (1.0 + gt) * acc
    o_ref[0] = _layer_norm(y, g_ref[...], b_ref[...])


def _ffn(x, mod, wup, cw, cb, wdn, g, b, tm, fc):
    bsz, s, d = x.shape
    f2 = wup.shape[1]
    return pl.pallas_call(
        functools.partial(_ffn_kernel, tm=tm, fc=fc),
        grid=(bsz, s // tm),
        in_specs=[pl.BlockSpec((1, tm, d), lambda b_, i: (b_, i, 0)),
                  pl.BlockSpec((1, N_MOD, d), lambda b_, i: (b_, 0, 0)),
                  _resident(wup.shape), _resident(cw.shape), _resident(cb.shape), _resident(wdn.shape),
                  _resident(g.shape), _resident(b.shape)],
        out_specs=pl.BlockSpec((1, tm, d), lambda b_, i: (b_, i, 0)),
        out_shape=jax.ShapeDtypeStruct((bsz, s, d), F32),
        scratch_shapes=[pltpu.VMEM((SUBLANES, f2), F32),
                        pltpu.VMEM((SUBLANES + tm, fc), F32)],
        compiler_params=_cparams(2),
        name="conv_ffn_ln",
    )(x, mod, wup, cw, cb, wdn, g, b)


def _tiles(s):
    return dict(
        proj_rows=min(512, s),
        fox_queries=min(1024, s),
        fox_keys=min(256, s),
        sb_block=LANES,
        sb_heads=N_HEADS,
        cumsum_chunk=min(512, s),
        ffn_rows=min(512, s),
    )


def kernel(x, c, w_ada, b_ada, w_in, b_forget, w_fox_proj, w_sb_proj, w_o, ln1_g, ln1_b,
           w_up, conv_w, conv_b, w_down, ln2_g, ln2_b):
    bsz, s, d = x.shape
    d_ff = w_down.shape[0]
    tl = _tiles(s)
    aw = ATTN_WIDTH

    qscale = jnp.concatenate([jnp.full((aw,), ATTN_SCALE, F32), jnp.ones((2 * aw,), F32)])
    wa = (w_in[:, 0:3 * aw] * qscale).astype(BF16)
    wf = jnp.pad(w_in[:, 3 * aw:3 * aw + N_HEADS], ((0, 0), (0, LANES - N_HEADS))).astype(BF16)
    b0 = 3 * aw + N_HEADS
    wb = (w_in[:, b0:b0 + 3 * aw] * qscale).astype(BF16)
    wg = w_in[:, b0 + 3 * aw:].astype(BF16)

    for _ in range(DEPTH):
        mod = _ada(c, w_ada, b_ada).reshape(bsz, N_MOD, d)
        qkva, qkvb, logf = _inproj(x, mod, wa, wb, wf, jnp.pad(b_forget, (0, LANES - N_HEADS)).reshape(1, LANES), tl["proj_rows"])
        cum = _cumsum_rows(logf.reshape(bsz * N_HEADS, s), tl["cumsum_chunk"]).reshape(bsz, N_HEADS, s)
        qx, kx, vt, qn2, kn2 = _fox_prep(qkva, cum, tl["proj_rows"])
        tq, tk = tl["fox_queries"], tl["fox_keys"]
        qn2 = jnp.max(qn2[..., 0].reshape(bsz, s // tq, tq // tl["proj_rows"], N_HEADS), axis=2)
        kn2 = jnp.max(kn2[..., 0], axis=1)
        qkmax = jnp.sqrt(jnp.transpose(qn2, (0, 2, 1)) * kn2[:, :, None])
        clast = cum[:, :, tk - 1::tk]
        y_fox = _fox_attention(qx, kx, vt, clast.reshape(-1), qkmax.reshape(-1), tq, tk)
        y_fox = y_fox.reshape(bsz, ATTN_WIDTH, s)
        y_sb = _sb_attention(qkvb, tl["sb_block"], tl["sb_heads"])
        x = _outproj(x, mod, y_fox, y_sb, wg, w_fox_proj.astype(BF16), w_sb_proj.astype(BF16),
                     w_o.astype(BF16), ln1_g.reshape(1, d), ln1_b.reshape(1, d), tl["proj_rows"])
        x = _ffn(x, mod, w_up.astype(BF16), conv_w, conv_b.reshape(1, 2 * d_ff), w_down.astype(BF16),
                 ln2_g.reshape(1, d), ln2_b.reshape(1, d), tl["ffn_rows"], d_ff // 2)
    return x
```

```python
import functools

import jax
import jax.numpy as jnp
from jax import lax
from jax.experimental import pallas as pl
from jax.experimental.pallas import tpu as pltpu

F32 = jnp.float32
BF16 = jnp.bfloat16

HEAD_DIM = 64
N_HEADS = 8
ATTN_WIDTH = N_HEADS * HEAD_DIM
N_MOD = 6
CONV_WIDTH = 3
LN_EPS = 1e-5
DEPTH = 1
DEEPNORM_ALPHA = (2.0 * DEPTH) ** 0.25
ATTN_SCALE = HEAD_DIM ** -0.5

LANES = 128
SUBLANES = 8
HEADS_PER_VREG_ROW = LANES // HEAD_DIM
VMEM_LIMIT = 56 * 1024 * 1024

EXP_ZERO_BELOW = -110.0


def _cparams(n_axes):
    return pltpu.CompilerParams(
        dimension_semantics=("arbitrary",) * n_axes, vmem_limit_bytes=VMEM_LIMIT)


def _resident(shape):
    nd = len(shape)
    return pl.BlockSpec(shape, lambda *_: (0,) * nd, pipeline_mode=pl.Buffered(1))


def _log_sigmoid(x):
    return jnp.minimum(x, 0.0) - jnp.log1p(jnp.exp(-jnp.abs(x)))


def _layer_norm(y, g, b):
    mu = jnp.mean(y, axis=-1, keepdims=True)
    d = y - mu
    var = jnp.mean(d * d, axis=-1, keepdims=True)
    return d * lax.rsqrt(var + LN_EPS) * g + b


def _ada_kernel(c_ref, w_ref, b_ref, o_ref):
    o_ref[...] = jnp.dot(c_ref[...], w_ref[...], preferred_element_type=F32,
                         precision=lax.Precision.HIGHEST) + b_ref[...]


def _ada(c, w_ada, b_ada):
    bsz, d = c.shape
    n = w_ada.shape[1]
    tn = n // 4
    return pl.pallas_call(
        _ada_kernel,
        grid=(n // tn,),
        in_specs=[pl.BlockSpec((bsz, d), lambda j: (0, 0)),
                  pl.BlockSpec((d, tn), lambda j: (0, j)),
                  pl.BlockSpec((1, tn), lambda j: (0, j))],
        out_specs=pl.BlockSpec((bsz, tn), lambda j: (0, j)),
        out_shape=jax.ShapeDtypeStruct((bsz, n), F32),
        compiler_params=_cparams(1),
        name="adaln_mod",
    )(c, w_ada, b_ada.reshape(1, n))


N_DECAY_TERMS = 3


def _split3(x):
    hi = x.astype(BF16)
    r = x - hi.astype(F32)
    mid = r.astype(BF16)
    lo = (r - mid.astype(F32)).astype(BF16)
    return hi, mid, lo


def _inproj_kernel(x_ref, mod_ref, wa_ref, wb_ref, wf_ref, bf_ref, tril_ref,
                   qx_ref, kx_ref, vt_ref, qn_ref, kn_ref, clast_ref, qkvb_ref, carry_ref):
    i = pl.program_id(1)
    tm = x_ref.shape[1]

    @pl.when(i == 0)
    def _():
        carry_ref[...] = jnp.zeros(carry_ref.shape, F32)

    x = x_ref[0]
    sh = mod_ref[0, 0:1, :]
    sc = mod_ref[0, 1:2, :]
    u = (x * (1.0 + sc) + sh).astype(BF16)
    f = jnp.dot(u, wf_ref[...], preferred_element_type=F32)
    lane = lax.broadcasted_iota(jnp.int32, (tm, LANES), 1)
    logf = jnp.where(lane < N_HEADS, _log_sigmoid(f + bf_ref[...]), 0.0)
    parts = jnp.dot(tril_ref[...], jnp.concatenate(_split3(logf), axis=1),
                    preferred_element_type=F32)
    cum = (parts[:, :LANES] + parts[:, LANES:2 * LANES] + parts[:, 2 * LANES:]) + carry_ref[...]
    carry_ref[...] = cum[tm - 1:tm, :]
    clast_ref[0, 0] = cum[tm - 1:tm, :]
    terms = [t.astype(F32) for t in _split3(-cum)]
    qkva = jnp.dot(u, wa_ref[...], preferred_element_type=F32).astype(BF16)
    qkvb_ref[0] = jnp.dot(u, wb_ref[...], preferred_element_type=F32).astype(BF16)

    def max_sq_norm(v):
        rows = jnp.sum(v * v, axis=1, keepdims=True)
        return jnp.broadcast_to(jnp.max(rows, axis=0, keepdims=True), (1, LANES))

    n_pairs = N_HEADS // HEADS_PER_VREG_ROW
    qns, kns = [], []
    for p in range(n_pairs):
        qp = qkva[:, p * LANES:(p + 1) * LANES].astype(F32)
        kp = qkva[:, (n_pairs + p) * LANES:(n_pairs + p + 1) * LANES].astype(F32)
        vp = qkva[:, (2 * n_pairs + p) * LANES:(2 * n_pairs + p + 1) * LANES].astype(F32)
        vt = jnp.transpose(vp).astype(BF16)
        for r in range(HEADS_PER_VREG_ROW):
            h = HEADS_PER_VREG_ROW * p + r
            own = (lane >= r * HEAD_DIM) & (lane < (r + 1) * HEAD_DIM)
            spare0 = (1 - r) * HEAD_DIM
            qx = jnp.where(own, qp, 0.0)
            kx = jnp.where(own, kp, 0.0)
            qns.append(max_sq_norm(qx))
            kns.append(max_sq_norm(kx))
            for n in range(N_DECAY_TERMS):
                at = lane == spare0 + n
                qx = jnp.where(at, 1.0, qx)
                kx = jnp.where(at, terms[n][:, h:h + 1], kx)
            qx_ref[0, h] = qx.astype(BF16)
            kx_ref[0, h] = kx.astype(BF16)
            vt_ref[0, h] = vt[r * HEAD_DIM:(r + 1) * HEAD_DIM, :]
    qn_ref[0, 0] = jnp.concatenate(qns, axis=0)
    kn_ref[0, 0] = jnp.concatenate(kns, axis=0)


def _inproj(x, mod, wa, wb, wf, bfg, tm):
    bsz, s, d = x.shape
    nt = s // tm
    row = lax.broadcasted_iota(jnp.int32, (tm, tm), 0)
    col = lax.broadcasted_iota(jnp.int32, (tm, tm), 1)
    tril = (col <= row).astype(BF16)
    head_rows = pl.BlockSpec((1, N_HEADS, tm, LANES), lambda b, i: (b, 0, i, 0))
    per_tile = pl.BlockSpec((1, 1, N_HEADS, LANES), lambda b, i: (b, i, 0, 0))
    return pl.pallas_call(
        _inproj_kernel,
        grid=(bsz, nt),
        in_specs=[pl.BlockSpec((1, tm, d), lambda b, i: (b, i, 0)),
                  pl.BlockSpec((1, N_MOD, d), lambda b, i: (b, 0, 0)),
                  _resident(wa.shape), _resident(wb.shape), _resident(wf.shape),
                  _resident(bfg.shape), _resident(tril.shape)],
        out_specs=[head_rows, head_rows,
                   pl.BlockSpec((1, N_HEADS, HEAD_DIM, tm), lambda b, i: (b, 0, 0, i)),
                   per_tile, per_tile,
                   pl.BlockSpec((1, 1, 1, LANES), lambda b, i: (b, i, 0, 0)),
                   pl.BlockSpec((1, tm, wb.shape[1]), lambda b, i: (b, i, 0))],
        out_shape=[jax.ShapeDtypeStruct((bsz, N_HEADS, s, LANES), BF16),
                   jax.ShapeDtypeStruct((bsz, N_HEADS, s, LANES), BF16),
                   jax.ShapeDtypeStruct((bsz, N_HEADS, HEAD_DIM, s), BF16),
                   jax.ShapeDtypeStruct((bsz, nt, N_HEADS, LANES), F32),
                   jax.ShapeDtypeStruct((bsz, nt, N_HEADS, LANES), F32),
                   jax.ShapeDtypeStruct((bsz, nt, 1, LANES), F32),
                   jax.ShapeDtypeStruct((bsz, s, wb.shape[1]), BF16)],
        scratch_shapes=[pltpu.VMEM((1, LANES), F32)],
        compiler_params=_cparams(2),
        name="in_proj",
    )(x, mod, wa, wb, wf, bfg, tril)


def _fox_kernel(clast_ref, qkmax_ref, qx_ref, kx_ref, vt_ref, o_ref,
                m_ref, l_ref, acc_ref, st_ref, *, tq, tk):
    b = pl.program_id(0)
    h = pl.program_id(1)
    i = pl.program_id(2)
    n_diag = tq // tk
    n_qblk = pl.num_programs(2)
    bh = b * pl.num_programs(1) + h
    m_ref[...] = jnp.full(m_ref.shape, -jnp.inf, F32)
    l_ref[...] = jnp.zeros(l_ref.shape, F32)
    acc_ref[...] = jnp.zeros(acc_ref.shape, F32)

    def scores(jb, c0):
        row0 = pl.multiple_of(jb * tk, tk)
        return lax.dot_general(kx_ref[0, 0, pl.ds(row0, tk), :], qx_ref[0, 0, c0:, :],
                               (((1,), (1,)), ((), ())), preferred_element_type=F32)

    def softmax_pv(jb, c0, diagonal):
        st = st_ref[:, c0:]
        if diagonal:
            key = lax.broadcasted_iota(jnp.int32, st.shape, 0)
            qry = lax.broadcasted_iota(jnp.int32, st.shape, 1)
            st = jnp.where(key <= qry, st, -jnp.inf)
        m_prev = m_ref[:, c0:]
        m_new = jnp.maximum(m_prev, jnp.max(st, axis=0, keepdims=True))
        alpha = jnp.exp(m_prev - m_new)
        pe = jnp.exp(st - m_new)
        l_ref[:, c0:] = alpha * l_ref[:, c0:] + jnp.sum(pe, axis=0, keepdims=True)
        row0 = pl.multiple_of(jb * tk, tk)
        vblk = vt_ref[0, 0, :, pl.ds(row0, tk)]
        acc_ref[:, c0:] = alpha * acc_ref[:, c0:] + jnp.dot(vblk, pe.astype(BF16),
                                                            preferred_element_type=F32)
        m_ref[:, c0:] = m_new

    first = i * n_diag
    st_ref[:, (n_diag - 1) * tk:] = scores(first + n_diag - 1, (n_diag - 1) * tk)
    for d in range(n_diag - 1, 0, -1):
        nxt = scores(first + d - 1, (d - 1) * tk)
        softmax_pv(first + d, d * tk, True)
        st_ref[:, (d - 1) * tk:] = nxt
    nxt = scores(jnp.maximum(first - 1, 0), 0)
    softmax_pv(first, 0, True)
    st_ref[...] = nxt

    bound_qk = qkmax_ref[bh * n_qblk + i]
    n_kblk = n_qblk * n_diag

    m_min = jnp.min(m_ref[...])

    def cond(j):
        reach = bound_qk - clast_ref[bh * n_kblk + jnp.maximum(j, 0)] - m_min
        return jnp.logical_and(j >= 0, reach >= EXP_ZERO_BELOW)

    def body(j):
        nxt = scores(jnp.maximum(j - 1, 0), 0)
        softmax_pv(j, 0, False)
        st_ref[...] = nxt
        return j - 1

    lax.while_loop(cond, body, first - 1)
    o_ref[0, 0] = (acc_ref[...] / l_ref[...]).astype(o_ref.dtype)


def _fox_attention(qx, kx, vt, clast, qkmax, tq, tk):
    bsz, nh, s, _ = qx.shape
    grid_spec = pltpu.PrefetchScalarGridSpec(
        num_scalar_prefetch=2,
        grid=(bsz, nh, s // tq),
        in_specs=[pl.BlockSpec((1, 1, tq, LANES), lambda b, h, i, *_: (b, h, i, 0)),
                  pl.BlockSpec((1, 1, s, LANES), lambda b, h, i, *_: (b, h, 0, 0)),
                  pl.BlockSpec((1, 1, HEAD_DIM, s), lambda b, h, i, *_: (b, h, 0, 0))],
        out_specs=pl.BlockSpec((1, 1, HEAD_DIM, tq), lambda b, h, i, *_: (b, h, 0, i)),
        scratch_shapes=[pltpu.VMEM((1, tq), F32),
                        pltpu.VMEM((1, tq), F32),
                        pltpu.VMEM((HEAD_DIM, tq), F32),
                        pltpu.VMEM((tk, tq), F32)])
    return pl.pallas_call(
        functools.partial(_fox_kernel, tq=tq, tk=tk),
        grid_spec=grid_spec,
        out_shape=jax.ShapeDtypeStruct((bsz, nh, HEAD_DIM, s), BF16),
        compiler_params=_cparams(3),
        name="fox_attention",
    )(clast, qkmax, qx, kx, vt)


def _sb_kernel(q_ref, k_ref, v_ref, u_ref, o_ref, acc_ref, carry_ref, *, t, heads, window):
    i = pl.program_id(2)
    n_pairs = heads // HEADS_PER_VREG_ROW
    lane = lax.broadcasted_iota(jnp.int32, (t, LANES), 1)
    qs = []
    for p in range(n_pairs):
        q = q_ref[0, :, p * LANES:(p + 1) * LANES]
        zero = jnp.zeros_like(q)
        qs.append((jnp.where(lane < HEAD_DIM, q, zero), jnp.where(lane >= HEAD_DIM, q, zero)))

    acc_ref[...] = jnp.zeros(acc_ref.shape, F32)
    carry_ref[...] = jnp.zeros(carry_ref.shape, F32)

    def visit(blocks):
        kbs, vbs = [], []
        for j, _ in blocks:
            col0 = pl.multiple_of(jnp.maximum(j, 0) * t, t)
            kbs.append(k_ref[0, pl.ds(col0, t), :])
            vbs.append(v_ref[0, pl.ds(col0, t), :])
        zs = [[None] * heads for _ in blocks]
        for w in range(len(blocks)):
            for h in range(heads):
                p, r = divmod(h, HEADS_PER_VREG_ROW)
                zs[w][h] = lax.dot_general(qs[p][r], kbs[w][:, p * LANES:(p + 1) * LANES],
                                           (((1,), (1,)), ((), ())), preferred_element_type=F32)
        reds = [[None] * heads for _ in blocks]
        for w, (_, keep) in enumerate(blocks):
            for h in range(heads):
                z = zs[w][h]
                sp = jnp.maximum(z, 0.0) + jnp.log(1.0 + jnp.exp(-jnp.abs(z)))
                if keep is not None:
                    sp = jnp.where(keep, sp, 0.0)
                hi = sp.astype(BF16)
                lo = (sp - hi.astype(F32)).astype(BF16)
                reds[w][h] = jnp.dot(jnp.concatenate([hi, lo], axis=1), u_ref[...],
                                     preferred_element_type=F32)
        worst = None
        for h in range(heads):
            p = h // HEADS_PER_VREG_ROW
            carry = carry_ref[h]
            total = None
            for w, (_, keep) in enumerate(blocks):
                a = jnp.exp(zs[w][h] + reds[w][h][:, :t] + carry)
                if keep is not None:
                    a = jnp.where(keep, a, 0.0)
                pv = jnp.dot(a.astype(BF16), vbs[w][:, p * LANES:(p + 1) * LANES],
                             preferred_element_type=F32)
                total = pv if total is None else total + pv
                carry = carry + reds[w][h][:, t:]
            acc_ref[h] += total
            carry_ref[h] = carry
            worst = carry if worst is None else jnp.maximum(worst, carry)
        return jnp.max(worst)

    strict = (lax.broadcasted_iota(jnp.int32, (t, t), 1) < lax.broadcasted_iota(jnp.int32, (t, t), 0))
    worst0 = visit([(i, strict)] + [(i - w, i - w >= 0) for w in range(1, window)])

    def cond(state):
        j, worst = state
        return jnp.logical_and(j >= 0, worst >= EXP_ZERO_BELOW)

    def body(state):
        j, _ = state
        return j - 1, visit([(j, None)])

    lax.while_loop(cond, body, (i - window, worst0))

    for p in range(n_pairs):
        o_ref[0, :, p * LANES:(p + 1) * LANES] = jnp.where(
            lane < HEAD_DIM, acc_ref[HEADS_PER_VREG_ROW * p], acc_ref[HEADS_PER_VREG_ROW * p + 1]
        ).astype(o_ref.dtype)


def _sb_attention(qkv, t, heads, window):
    bsz, s, _ = qkv.shape
    n_groups = N_HEADS // heads
    w = heads * HEAD_DIM
    j = lax.broadcasted_iota(jnp.int32, (2 * t, 2 * t), 0) % t
    c = lax.broadcasted_iota(jnp.int32, (2 * t, 2 * t), 1)
    u = -jnp.logical_or(c >= t, j >= c).astype(BF16)
    return pl.pallas_call(
        functools.partial(_sb_kernel, t=t, heads=heads, window=window),
        grid=(bsz, n_groups, s // t),
        in_specs=[pl.BlockSpec((1, t, w), lambda b, g, i: (b, i, g)),
                  pl.BlockSpec((1, s, w), lambda b, g, i: (b, 0, n_groups + g)),
                  pl.BlockSpec((1, s, w), lambda b, g, i: (b, 0, 2 * n_groups + g)),
                  pl.BlockSpec(u.shape, lambda b, g, i: (0, 0))],
        out_specs=pl.BlockSpec((1, t, w), lambda b, g, i: (b, i, g)),
        out_shape=jax.ShapeDtypeStruct((bsz, s, ATTN_WIDTH), BF16),
        scratch_shapes=[pltpu.VMEM((heads, t, LANES), F32),
                        pltpu.VMEM((heads, t, LANES), F32)],
        compiler_params=_cparams(3),
        name="stickbreak_attention",
    )(qkv, qkv, qkv, u)


def _outproj_kernel(x_ref, mod_ref, yf_ref, ys_ref, wg_ref, wfp_ref, wsp_ref, wo_ref, g_ref, b_ref, o_ref):
    x = x_ref[0]
    d = x.shape[-1]
    sh = mod_ref[0, 0:1, :]
    sc = mod_ref[0, 1:2, :]
    gt = mod_ref[0, 2:3, :]
    u = (x * (1.0 + sc) + sh).astype(BF16)
    gates = jnp.dot(u, wg_ref[...], preferred_element_type=F32)
    pa = lax.dot_general(yf_ref[0], wfp_ref[...], (((0,), (0,)), ((), ())), preferred_element_type=F32)
    pb = jnp.dot(ys_ref[0], wsp_ref[...], preferred_element_type=F32)
    merged = jax.nn.sigmoid(gates[:, :d]) * pa + jax.nn.sigmoid(gates[:, d:]) * pb
    attn = jnp.dot(merged.astype(BF16), wo_ref[...], preferred_element_type=F32)
    y = DEEPNORM_ALPHA * x + (1.0 + gt) * attn
    o_ref[0] = _layer_norm(y, g_ref[...], b_ref[...])


def _outproj(x, mod, yf, ys, wg, wfp, wsp, wo, g, b, tm):
    bsz, s, d = x.shape
    aw = ys.shape[-1]
    return pl.pallas_call(
        _outproj_kernel,
        grid=(bsz, s // tm),
        in_specs=[pl.BlockSpec((1, tm, d), lambda b_, i: (b_, i, 0)),
                  pl.BlockSpec((1, N_MOD, d), lambda b_, i: (b_, 0, 0)),
                  pl.BlockSpec((1, aw, tm), lambda b_, i: (b_, 0, i)),
                  pl.BlockSpec((1, tm, aw), lambda b_, i: (b_, i, 0)),
                  _resident(wg.shape), _resident(wfp.shape), _resident(wsp.shape), _resident(wo.shape),
                  _resident(g.shape), _resident(b.shape)],
        out_specs=pl.BlockSpec((1, tm, d), lambda b_, i: (b_, i, 0)),
        out_shape=jax.ShapeDtypeStruct((bsz, s, d), F32),
        compiler_params=_cparams(2),
        name="merge_out_proj_ln",
    )(x, mod, yf, ys, wg, wfp, wsp, wo, g, b)


def _ffn_kernel(x_ref, mod_ref, wup_ref, cw_ref, cb_ref, wdn_ref, g_ref, b_ref, o_ref,
                tail_ref, hbuf_ref, *, tm, fc):
    i = pl.program_id(1)
    d_ff = wdn_ref.shape[0]
    halo = SUBLANES

    @pl.when(i == 0)
    def _():
        tail_ref[...] = jnp.zeros(tail_ref.shape, F32)

    x = x_ref[0]
    sh = mod_ref[0, 3:4, :]
    sc = mod_ref[0, 4:5, :]
    gt = mod_ref[0, 5:6, :]
    u = (x * (1.0 + sc) + sh).astype(BF16)

    def conv_part(col0):
        h = jnp.dot(u, wup_ref[:, col0:col0 + fc], preferred_element_type=F32)
        hbuf_ref[0:halo, :] = tail_ref[:, col0:col0 + fc]
        hbuf_ref[halo:halo + tm, :] = h
        tail_ref[:, col0:col0 + fc] = h[tm - halo:tm, :]
        h1 = hbuf_ref[halo - 1:halo - 1 + tm, :]
        h2 = hbuf_ref[halo - 2:halo - 2 + tm, :]
        w = cw_ref[:, col0:col0 + fc]
        out = cb_ref[:, col0:col0 + fc] + h2 * w[0:1, :]
        out = out + h1 * w[1:2, :]
        return out + h * w[2:3, :]

    acc = jnp.zeros(x.shape, F32)
    for c in range(d_ff // fc):
        hg = conv_part(c * fc)
        hv = conv_part(d_ff + c * fc)
        act = (hg * jax.nn.sigmoid(hg)) * hv
        acc = acc + jnp.dot(act.astype(BF16), wdn_ref[c * fc:(c + 1) * fc, :], preferred_element_type=F32)

    y = DEEPNORM_ALPHA * x + (1.0 + gt) * acc
    o_ref[0] = _layer_norm(y, g_ref[...], b_ref[...])


def _ffn(x, mod, wup, cw, cb, wdn, g, b, tm, fc):
    bsz, s, d = x.shape
    f2 = wup.shape[1]
    return pl.pallas_call(
        functools.partial(_ffn_kernel, tm=tm, fc=fc),
        grid=(bsz, s // tm),
        in_specs=[pl.BlockSpec((1, tm, d), lambda b_, i: (b_, i, 0)),
                  pl.BlockSpec((1, N_MOD, d), lambda b_, i: (b_, 0, 0)),
                  _resident(wup.shape), _resident(cw.shape), _resident(cb.shape), _resident(wdn.shape),
                  _resident(g.shape), _resident(b.shape)],
        out_specs=pl.BlockSpec((1, tm, d), lambda b_, i: (b_, i, 0)),
        out_shape=jax.ShapeDtypeStruct((bsz, s, d), F32),
        scratch_shapes=[pltpu.VMEM((SUBLANES, f2), F32),
                        pltpu.VMEM((SUBLANES + tm, fc), F32)],
        compiler_params=_cparams(2),
        name="conv_ffn_ln",
    )(x, mod, wup, cw, cb, wdn, g, b)


def _tiles(s):
    return dict(
        proj_rows=min(512, s),
        fox_queries=min(1024, s),
        fox_keys=min(512, s),
        sb_block=LANES,
        sb_heads=N_HEADS,
        sb_window=3,
        ffn_rows=min(512, s),
    )


def kernel(x, c, w_ada, b_ada, w_in, b_forget, w_fox_proj, w_sb_proj, w_o, ln1_g, ln1_b,
           w_up, conv_w, conv_b, w_down, ln2_g, ln2_b):
    bsz, s, d = x.shape
    d_ff = w_down.shape[0]
    tl = _tiles(s)
    aw = ATTN_WIDTH

    qscale = jnp.concatenate([jnp.full((aw,), ATTN_SCALE, F32), jnp.ones((2 * aw,), F32)])
    wa = (w_in[:, 0:3 * aw] * qscale).astype(BF16)
    wf = jnp.pad(w_in[:, 3 * aw:3 * aw + N_HEADS], ((0, 0), (0, LANES - N_HEADS))).astype(BF16)
    b0 = 3 * aw + N_HEADS
    wb = (w_in[:, b0:b0 + 3 * aw] * qscale).astype(BF16)
    wg = w_in[:, b0 + 3 * aw:].astype(BF16)

    for _ in range(DEPTH):
        mod = _ada(c, w_ada, b_ada).reshape(bsz, N_MOD, d)
        b_f = jnp.pad(b_forget, (0, LANES - N_HEADS)).reshape(1, LANES)
        qx, kx, vt, qn2, kn2, clast, qkvb = _inproj(x, mod, wa, wb, wf, b_f, tl["proj_rows"])
        tq, tk = tl["fox_queries"], tl["fox_keys"]
        assert tk == tl["proj_rows"] and tq % tk == 0
        qn2 = jnp.max(qn2[..., 0].reshape(bsz, s // tq, tq // tk, N_HEADS), axis=2)
        kn2 = jnp.max(kn2[..., 0], axis=1)
        qkmax = jnp.sqrt(jnp.transpose(qn2, (0, 2, 1)) * kn2[:, :, None])
        clast = jnp.transpose(clast[:, :, 0, :N_HEADS], (0, 2, 1))
        y_fox = _fox_attention(qx, kx, vt, clast.reshape(-1), qkmax.reshape(-1), tq, tk)
        y_fox = y_fox.reshape(bsz, ATTN_WIDTH, s)
        y_sb = _sb_attention(qkvb, tl["sb_block"], tl["sb_heads"], tl["sb_window"])
        x = _outproj(x, mod, y_fox, y_sb, wg, w_fox_proj.astype(BF16), w_sb_proj.astype(BF16),
                     w_o.astype(BF16), ln1_g.reshape(1, d), ln1_b.reshape(1, d), tl["proj_rows"])
        x = _ffn(x, mod, w_up.astype(BF16), conv_w, conv_b.reshape(1, 2 * d_ff), w_down.astype(BF16),
                 ln2_g.reshape(1, d), ln2_b.reshape(1, d), tl["ffn_rows"], d_ff // 2)
    return x
```

```python
import functools

import jax
import jax.numpy as jnp
from jax import lax
from jax.experimental import pallas as pl
from jax.experimental.pallas import tpu as pltpu

F32 = jnp.float32
BF16 = jnp.bfloat16

HEAD_DIM = 64
N_HEADS = 8
ATTN_WIDTH = N_HEADS * HEAD_DIM
N_MOD = 6
CONV_WIDTH = 3
LN_EPS = 1e-5
DEPTH = 1
DEEPNORM_ALPHA = (2.0 * DEPTH) ** 0.25
ATTN_SCALE = HEAD_DIM ** -0.5

LANES = 128
SUBLANES = 8
HEADS_PER_VREG_ROW = LANES // HEAD_DIM
VMEM_LIMIT = 56 * 1024 * 1024

EXP_ZERO_BELOW = -110.0


def _cparams(n_axes):
    return pltpu.CompilerParams(
        dimension_semantics=("arbitrary",) * n_axes, vmem_limit_bytes=VMEM_LIMIT)


def _resident(shape):
    nd = len(shape)
    return pl.BlockSpec(shape, lambda *_: (0,) * nd, pipeline_mode=pl.Buffered(1))


def _log_sigmoid(x):
    return jnp.minimum(x, 0.0) - jnp.log1p(jnp.exp(-jnp.abs(x)))


def _layer_norm(y, g, b):
    mu = jnp.mean(y, axis=-1, keepdims=True)
    d = y - mu
    var = jnp.mean(d * d, axis=-1, keepdims=True)
    return d * lax.rsqrt(var + LN_EPS) * g + b


def _ada_kernel(c_ref, w_ref, b_ref, o_ref):
    o_ref[...] = jnp.dot(c_ref[...], w_ref[...], preferred_element_type=F32,
                         precision=lax.Precision.HIGHEST) + b_ref[...]


def _ada(c, w_ada, b_ada):
    bsz, d = c.shape
    n = w_ada.shape[1]
    tn = n // 4
    return pl.pallas_call(
        _ada_kernel,
        grid=(n // tn,),
        in_specs=[pl.BlockSpec((bsz, d), lambda j: (0, 0)),
                  pl.BlockSpec((d, tn), lambda j: (0, j)),
                  pl.BlockSpec((1, tn), lambda j: (0, j))],
        out_specs=pl.BlockSpec((bsz, tn), lambda j: (0, j)),
        out_shape=jax.ShapeDtypeStruct((bsz, n), F32),
        compiler_params=_cparams(1),
        name="adaln_mod",
    )(c, w_ada, b_ada.reshape(1, n))


N_DECAY_TERMS = 3


def _split3(x):
    hi = x.astype(BF16)
    r = x - hi.astype(F32)
    mid = r.astype(BF16)
    lo = (r - mid.astype(F32)).astype(BF16)
    return hi, mid, lo


def _inproj_kernel(x_ref, mod_ref, wa_ref, wb_ref, wf_ref, bf_ref, tril_ref,
                   qx_ref, kx_ref, vt_ref, qn_ref, kn_ref, clast_ref, qkvb_ref, carry_ref):
    i = pl.program_id(1)
    tm = x_ref.shape[1]

    @pl.when(i == 0)
    def _():
        carry_ref[...] = jnp.zeros(carry_ref.shape, F32)

    x = x_ref[0]
    sh = mod_ref[0, 0:1, :]
    sc = mod_ref[0, 1:2, :]
    u = (x * (1.0 + sc) + sh).astype(BF16)
    f = jnp.dot(u, wf_ref[...], preferred_element_type=F32)
    lane = lax.broadcasted_iota(jnp.int32, (tm, LANES), 1)
    logf = jnp.where(lane < N_HEADS, _log_sigmoid(f + bf_ref[...]), 0.0)
    parts = jnp.dot(tril_ref[...], jnp.concatenate(_split3(logf), axis=1),
                    preferred_element_type=F32)
    cum = (parts[:, :LANES] + parts[:, LANES:2 * LANES] + parts[:, 2 * LANES:]) + carry_ref[...]
    carry_ref[...] = cum[tm - 1:tm, :]
    clast_ref[0, 0] = cum[tm - 1:tm, :]
    terms = [t.astype(F32) for t in _split3(-cum)]
    qkva = jnp.dot(u, wa_ref[...], preferred_element_type=F32).astype(BF16)
    qkvb_ref[0] = jnp.dot(u, wb_ref[...], preferred_element_type=F32).astype(BF16)

    def max_sq_norm(v):
        rows = jnp.sum(v * v, axis=1, keepdims=True)
        return jnp.broadcast_to(jnp.max(rows, axis=0, keepdims=True), (1, LANES))

    n_pairs = N_HEADS // HEADS_PER_VREG_ROW
    qns, kns = [], []
    for p in range(n_pairs):
        qp = qkva[:, p * LANES:(p + 1) * LANES].astype(F32)
        kp = qkva[:, (n_pairs + p) * LANES:(n_pairs + p + 1) * LANES].astype(F32)
        vp = qkva[:, (2 * n_pairs + p) * LANES:(2 * n_pairs + p + 1) * LANES].astype(F32)
        vt = jnp.transpose(vp).astype(BF16)
        for r in range(HEADS_PER_VREG_ROW):
            h = HEADS_PER_VREG_ROW * p + r
            own = (lane >= r * HEAD_DIM) & (lane < (r + 1) * HEAD_DIM)
            spare0 = (1 - r) * HEAD_DIM
            qx = jnp.where(own, qp, 0.0)
            kx = jnp.where(own, kp, 0.0)
            qns.append(max_sq_norm(qx))
            kns.append(max_sq_norm(kx))
            for n in range(N_DECAY_TERMS):
                at = lane == spare0 + n
                qx = jnp.where(at, 1.0, qx)
                kx = jnp.where(at, terms[n][:, h:h + 1], kx)
            qx_ref[0, h] = qx.astype(BF16)
            kx_ref[0, h] = kx.astype(BF16)
            vt_ref[0, h] = vt[r * HEAD_DIM:(r + 1) * HEAD_DIM, :]
    qn_ref[0, 0] = jnp.concatenate(qns, axis=0)
    kn_ref[0, 0] = jnp.concatenate(kns, axis=0)


def _inproj(x, mod, wa, wb, wf, bfg, tm):
    bsz, s, d = x.shape
    nt = s // tm
    row = lax.broadcasted_iota(jnp.int32, (tm, tm), 0)
    col = lax.broadcasted_iota(jnp.int32, (tm, tm), 1)
    tril = (col <= row).astype(BF16)
    head_rows = pl.BlockSpec((1, N_HEADS, tm, LANES), lambda b, i: (b, 0, i, 0))
    per_tile = pl.BlockSpec((1, 1, N_HEADS, LANES), lambda b, i: (b, i, 0, 0))
    return pl.pallas_call(
        _inproj_kernel,
        grid=(bsz, nt),
        in_specs=[pl.BlockSpec((1, tm, d), lambda b, i: (b, i, 0)),
                  pl.BlockSpec((1, N_MOD, d), lambda b, i: (b, 0, 0)),
                  _resident(wa.shape), _resident(wb.shape), _resident(wf.shape),
                  _resident(bfg.shape), _resident(tril.shape)],
        out_specs=[head_rows, head_rows,
                   pl.BlockSpec((1, N_HEADS, HEAD_DIM, tm), lambda b, i: (b, 0, 0, i)),
                   per_tile, per_tile,
                   pl.BlockSpec((1, 1, 1, LANES), lambda b, i: (b, i, 0, 0)),
                   pl.BlockSpec((1, tm, wb.shape[1]), lambda b, i: (b, i, 0))],
        out_shape=[jax.ShapeDtypeStruct((bsz, N_HEADS, s, LANES), BF16),
                   jax.ShapeDtypeStruct((bsz, N_HEADS, s, LANES), BF16),
                   jax.ShapeDtypeStruct((bsz, N_HEADS, HEAD_DIM, s), BF16),
                   jax.ShapeDtypeStruct((bsz, nt, N_HEADS, LANES), F32),
                   jax.ShapeDtypeStruct((bsz, nt, N_HEADS, LANES), F32),
                   jax.ShapeDtypeStruct((bsz, nt, 1, LANES), F32),
                   jax.ShapeDtypeStruct((bsz, s, wb.shape[1]), BF16)],
        scratch_shapes=[pltpu.VMEM((1, LANES), F32)],
        compiler_params=_cparams(2),
        name="in_proj",
    )(x, mod, wa, wb, wf, bfg, tril)


def _fox_kernel(clast_ref, qkmax_ref, qx_ref, kx_ref, vt_ref, o_ref,
                m_ref, l_ref, acc_ref, st_ref, *, tq, tk):
    bh = pl.program_id(0) * pl.num_programs(1) + pl.program_id(1)
    n_diag = tq // tk
    n_qblk = qx_ref.shape[2] // tq
    n_kblk = n_qblk * n_diag

    def scores(jb, q0, c0):
        row0 = pl.multiple_of(jb * tk, tk)
        return lax.dot_general(kx_ref[0, 0, pl.ds(row0, tk), :],
                               qx_ref[0, 0, pl.ds(q0 + c0, tq - c0), :],
                               (((1,), (1,)), ((), ())), preferred_element_type=F32)

    def softmax_pv(jb, c0, diagonal):
        st = st_ref[:, c0:]
        if diagonal:
            key = lax.broadcasted_iota(jnp.int32, st.shape, 0)
            qry = lax.broadcasted_iota(jnp.int32, st.shape, 1)
            st = jnp.where(key <= qry, st, -jnp.inf)
        m_prev = m_ref[:, c0:]
        m_new = jnp.maximum(m_prev, jnp.max(st, axis=0, keepdims=True))
        alpha = jnp.exp(m_prev - m_new)
        pe = jnp.exp(st - m_new)
        l_ref[:, c0:] = alpha * l_ref[:, c0:] + jnp.sum(pe, axis=0, keepdims=True)
        row0 = pl.multiple_of(jb * tk, tk)
        vblk = vt_ref[0, 0, :, pl.ds(row0, tk)]
        acc_ref[:, c0:] = alpha * acc_ref[:, c0:] + jnp.dot(vblk, pe.astype(BF16),
                                                            preferred_element_type=F32)
        m_ref[:, c0:] = m_new

    def query_block(i, carry):
        q0 = pl.multiple_of(i * tq, tq)
        m_ref[...] = jnp.full(m_ref.shape, -jnp.inf, F32)
        l_ref[...] = jnp.zeros(l_ref.shape, F32)
        acc_ref[...] = jnp.zeros(acc_ref.shape, F32)

        first = i * n_diag
        st_ref[:, (n_diag - 1) * tk:] = scores(first + n_diag - 1, q0, (n_diag - 1) * tk)
        for d in range(n_diag - 1, 0, -1):
            nxt = scores(first + d - 1, q0, (d - 1) * tk)
            softmax_pv(first + d, d * tk, True)
            st_ref[:, (d - 1) * tk:] = nxt
        nxt = scores(jnp.maximum(first - 1, 0), q0, 0)
        softmax_pv(first, 0, True)
        st_ref[...] = nxt

        bound_qk = qkmax_ref[bh * n_qblk + i]
        m_min = jnp.min(m_ref[...])

        def cond(j):
            reach = bound_qk - clast_ref[bh * n_kblk + jnp.maximum(j, 0)] - m_min
            return jnp.logical_and(j >= 0, reach >= EXP_ZERO_BELOW)

        def body(j):
            nxt = scores(jnp.maximum(j - 1, 0), q0, 0)
            softmax_pv(j, 0, False)
            st_ref[...] = nxt
            return j - 1

        lax.while_loop(cond, body, first - 1)
        o_ref[0, 0, :, pl.ds(q0, tq)] = (acc_ref[...] / l_ref[...]).astype(o_ref.dtype)
        return carry

    lax.fori_loop(0, n_qblk, query_block, 0)


def _fox_attention(qx, kx, vt, clast, qkmax, tq, tk):
    bsz, nh, s, _ = qx.shape
    grid_spec = pltpu.PrefetchScalarGridSpec(
        num_scalar_prefetch=2,
        grid=(bsz, nh),
        in_specs=[pl.BlockSpec((1, 1, s, LANES), lambda b, h, *_: (b, h, 0, 0)),
                  pl.BlockSpec((1, 1, s, LANES), lambda b, h, *_: (b, h, 0, 0)),
                  pl.BlockSpec((1, 1, HEAD_DIM, s), lambda b, h, *_: (b, h, 0, 0))],
        out_specs=pl.BlockSpec((1, 1, HEAD_DIM, s), lambda b, h, *_: (b, h, 0, 0)),
        scratch_shapes=[pltpu.VMEM((1, tq), F32),
                        pltpu.VMEM((1, tq), F32),
                        pltpu.VMEM((HEAD_DIM, tq), F32),
                        pltpu.VMEM((tk, tq), F32)])
    return pl.pallas_call(
        functools.partial(_fox_kernel, tq=tq, tk=tk),
        grid_spec=grid_spec,
        out_shape=jax.ShapeDtypeStruct((bsz, nh, HEAD_DIM, s), BF16),
        compiler_params=_cparams(2),
        name="fox_attention",
    )(clast, qkmax, qx, kx, vt)


def _sb_kernel(q_ref, k_ref, v_ref, u_ref, o_ref, acc_ref, carry_ref, *, t, heads, window):
    n_pairs = heads // HEADS_PER_VREG_ROW
    lane = lax.broadcasted_iota(jnp.int32, (t, LANES), 1)
    strict = (lax.broadcasted_iota(jnp.int32, (t, t), 1) < lax.broadcasted_iota(jnp.int32, (t, t), 0))

    def visit(qs, blocks):
        kbs, vbs = [], []
        for j, _ in blocks:
            col0 = pl.multiple_of(jnp.maximum(j, 0) * t, t)
            kbs.append(k_ref[0, pl.ds(col0, t), :])
            vbs.append(v_ref[0, pl.ds(col0, t), :])
        zs = [[None] * heads for _ in blocks]
        for w in range(len(blocks)):
            for h in range(heads):
                p, r = divmod(h, HEADS_PER_VREG_ROW)
                zs[w][h] = lax.dot_general(qs[p][r], kbs[w][:, p * LANES:(p + 1) * LANES],
                                           (((1,), (1,)), ((), ())), preferred_element_type=F32)
        reds = [[None] * heads for _ in blocks]
        for w, (_, keep) in enumerate(blocks):
            for h in range(heads):
                z = zs[w][h]
                sp = jnp.maximum(z, 0.0) + jnp.log(1.0 + jnp.exp(-jnp.abs(z)))
                if keep is not None:
                    sp = jnp.where(keep, sp, 0.0)
                hi = sp.astype(BF16)
                lo = (sp - hi.astype(F32)).astype(BF16)
                reds[w][h] = jnp.dot(jnp.concatenate([hi, lo], axis=1), u_ref[...],
                                     preferred_element_type=F32)
        worst = None
        for h in range(heads):
            p = h // HEADS_PER_VREG_ROW
            carry = carry_ref[h]
            total = None
            for w, (_, keep) in enumerate(blocks):
                a = jnp.exp(zs[w][h] + reds[w][h][:, :t] + carry)
                if keep is not None:
                    a = jnp.where(keep, a, 0.0)
                pv = jnp.dot(a.astype(BF16), vbs[w][:, p * LANES:(p + 1) * LANES],
                             preferred_element_type=F32)
                total = pv if total is None else total + pv
                carry = carry + reds[w][h][:, t:]
            acc_ref[h] += total
            carry_ref[h] = carry
            worst = carry if worst is None else jnp.maximum(worst, carry)
        return jnp.max(worst)

    n_sub = q_ref.shape[1] // t

    def query_block(sub, unused):
        i = pl.program_id(2) * n_sub + sub
        r0 = pl.multiple_of(sub * t, t)
        qs = []
        for p in range(n_pairs):
            q = q_ref[0, pl.ds(r0, t), p * LANES:(p + 1) * LANES]
            zero = jnp.zeros_like(q)
            qs.append((jnp.where(lane < HEAD_DIM, q, zero), jnp.where(lane >= HEAD_DIM, q, zero)))
        acc_ref[...] = jnp.zeros(acc_ref.shape, F32)
        carry_ref[...] = jnp.zeros(carry_ref.shape, F32)

        worst0 = visit(qs, [(i, strict)] + [(i - w, i - w >= 0) for w in range(1, window)])

        def cond(state):
            j, worst = state
            return jnp.logical_and(j >= 0, worst >= EXP_ZERO_BELOW)

        def body(state):
            j, _ = state
            return j - 1, visit(qs, [(j, None)])

        lax.while_loop(cond, body, (i - window, worst0))

        for p in range(n_pairs):
            o_ref[0, pl.ds(r0, t), p * LANES:(p + 1) * LANES] = jnp.where(
                lane < HEAD_DIM, acc_ref[HEADS_PER_VREG_ROW * p], acc_ref[HEADS_PER_VREG_ROW * p + 1]
            ).astype(o_ref.dtype)
        return unused

    lax.fori_loop(0, n_sub, query_block, 0)


def _sb_attention(qkv, t, heads, window, rows):
    bsz, s, _ = qkv.shape
    n_groups = N_HEADS // heads
    w = heads * HEAD_DIM
    j = lax.broadcasted_iota(jnp.int32, (2 * t, 2 * t), 0) % t
    c = lax.broadcasted_iota(jnp.int32, (2 * t, 2 * t), 1)
    u = -jnp.logical_or(c >= t, j >= c).astype(BF16)
    return pl.pallas_call(
        functools.partial(_sb_kernel, t=t, heads=heads, window=window),
        grid=(bsz, n_groups, s // rows),
        in_specs=[pl.BlockSpec((1, rows, w), lambda b, g, i: (b, i, g)),
                  pl.BlockSpec((1, s, w), lambda b, g, i: (b, 0, n_groups + g)),
                  pl.BlockSpec((1, s, w), lambda b, g, i: (b, 0, 2 * n_groups + g)),
                  pl.BlockSpec(u.shape, lambda b, g, i: (0, 0))],
        out_specs=pl.BlockSpec((1, rows, w), lambda b, g, i: (b, i, g)),
        out_shape=jax.ShapeDtypeStruct((bsz, s, ATTN_WIDTH), BF16),
        scratch_shapes=[pltpu.VMEM((heads, t, LANES), F32),
                        pltpu.VMEM((heads, t, LANES), F32)],
        compiler_params=_cparams(3),
        name="stickbreak_attention",
    )(qkv, qkv, qkv, u)


def _outproj_kernel(x_ref, mod_ref, yf_ref, ys_ref, wg_ref, wfp_ref, wsp_ref, wo_ref, g_ref, b_ref, o_ref):
    x = x_ref[0]
    d = x.shape[-1]
    sh = mod_ref[0, 0:1, :]
    sc = mod_ref[0, 1:2, :]
    gt = mod_ref[0, 2:3, :]
    u = (x * (1.0 + sc) + sh).astype(BF16)
    gates = jnp.dot(u, wg_ref[...], preferred_element_type=F32)
    pa = lax.dot_general(yf_ref[0], wfp_ref[...], (((0,), (0,)), ((), ())), preferred_element_type=F32)
    pb = jnp.dot(ys_ref[0], wsp_ref[...], preferred_element_type=F32)
    merged = jax.nn.sigmoid(gates[:, :d]) * pa + jax.nn.sigmoid(gates[:, d:]) * pb
    attn = jnp.dot(merged.astype(BF16), wo_ref[...], preferred_element_type=F32)
    y = DEEPNORM_ALPHA * x + (1.0 + gt) * attn
    o_ref[0] = _layer_norm(y, g_ref[...], b_ref[...])


def _outproj(x, mod, yf, ys, wg, wfp, wsp, wo, g, b, tm):
    bsz, s, d = x.shape
    aw = ys.shape[-1]
    return pl.pallas_call(
        _outproj_kernel,
        grid=(bsz, s // tm),
        in_specs=[pl.BlockSpec((1, tm, d), lambda b_, i: (b_, i, 0)),
                  pl.BlockSpec((1, N_MOD, d), lambda b_, i: (b_, 0, 0)),
                  pl.BlockSpec((1, aw, tm), lambda b_, i: (b_, 0, i)),
                  pl.BlockSpec((1, tm, aw), lambda b_, i: (b_, i, 0)),
                  _resident(wg.shape), _resident(wfp.shape), _resident(wsp.shape), _resident(wo.shape),
                  _resident(g.shape), _resident(b.shape)],
        out_specs=pl.BlockSpec((1, tm, d), lambda b_, i: (b_, i, 0)),
        out_shape=jax.ShapeDtypeStruct((bsz, s, d), F32),
        compiler_params=_cparams(2),
        name="merge_out_proj_ln",
    )(x, mod, yf, ys, wg, wfp, wsp, wo, g, b)


def _ffn_kernel(x_ref, mod_ref, wup_ref, cw_ref, cb_ref, wdn_ref, g_ref, b_ref, o_ref,
                tail_ref, hbuf_ref, *, tm, chunks):
    i = pl.program_id(1)
    d_ff = wdn_ref.shape[0]
    halo = SUBLANES

    @pl.when(i == 0)
    def _():
        tail_ref[...] = jnp.zeros(tail_ref.shape, F32)

    x = x_ref[0]
    sh = mod_ref[0, 3:4, :]
    sc = mod_ref[0, 4:5, :]
    gt = mod_ref[0, 5:6, :]
    u = (x * (1.0 + sc) + sh).astype(BF16)

    def up(col0, n):
        return jnp.dot(u, wup_ref[:, col0:col0 + n], preferred_element_type=F32)

    def conv(h, col0, n):
        hbuf_ref[0:halo, 0:n] = tail_ref[:, col0:col0 + n]
        hbuf_ref[halo:halo + tm, 0:n] = h
        tail_ref[:, col0:col0 + n] = h[tm - halo:tm, :]
        h1 = hbuf_ref[halo - 1:halo - 1 + tm, 0:n]
        h2 = hbuf_ref[halo - 2:halo - 2 + tm, 0:n]
        w = cw_ref[:, col0:col0 + n]
        out = cb_ref[:, col0:col0 + n] + h2 * w[0:1, :]
        out = out + h1 * w[1:2, :]
        return out + h * w[2:3, :]

    starts = [sum(chunks[:c]) for c in range(len(chunks))]
    ups = [None] * len(chunks)
    ups[0] = (up(starts[0], chunks[0]), up(d_ff + starts[0], chunks[0]))
    acc = jnp.zeros(x.shape, F32)
    for c, n in enumerate(chunks):
        if c + 1 < len(chunks):
            ups[c + 1] = (up(starts[c + 1], chunks[c + 1]), up(d_ff + starts[c + 1], chunks[c + 1]))
        hg = conv(ups[c][0], starts[c], n)
        hv = conv(ups[c][1], d_ff + starts[c], n)
        act = (hg * jax.nn.sigmoid(hg)) * hv
        acc = acc + jnp.dot(act.astype(BF16), wdn_ref[starts[c]:starts[c] + n, :],
                            preferred_element_type=F32)

    y = DEEPNORM_ALPHA * x + (1.0 + gt) * acc
    o_ref[0] = _layer_norm(y, g_ref[...], b_ref[...])


def _ffn(x, mod, wup, cw, cb, wdn, g, b, tm, chunks):
    bsz, s, d = x.shape
    f2 = wup.shape[1]
    assert sum(chunks) == wdn.shape[0]
    return pl.pallas_call(
        functools.partial(_ffn_kernel, tm=tm, chunks=chunks),
        grid=(bsz, s // tm),
        in_specs=[pl.BlockSpec((1, tm, d), lambda b_, i: (b_, i, 0)),
                  pl.BlockSpec((1, N_MOD, d), lambda b_, i: (b_, 0, 0)),
                  _resident(wup.shape), _resident(cw.shape), _resident(cb.shape), _resident(wdn.shape),
                  _resident(g.shape), _resident(b.shape)],
        out_specs=pl.BlockSpec((1, tm, d), lambda b_, i: (b_, i, 0)),
        out_shape=jax.ShapeDtypeStruct((bsz, s, d), F32),
        scratch_shapes=[pltpu.VMEM((SUBLANES, f2), F32),
                        pltpu.VMEM((SUBLANES + tm, max(chunks)), F32)],
        compiler_params=_cparams(2),
        name="conv_ffn_ln",
    )(x, mod, wup, cw, cb, wdn, g, b)


def _tiles(s):
    return dict(
        proj_rows=min(512, s),
        fox_queries=min(1024, s),
        fox_keys=min(512, s),
        sb_block=LANES,
        sb_heads=N_HEADS,
        sb_window=3,
        sb_rows=min(1024, s),
        ffn_rows=min(512, s),
    )


MXU_WIDTH = 256


def _ffn_chunks(d_ff):
    units = d_ff // MXU_WIDTH
    assert units * MXU_WIDTH == d_ff and units >= 5
    mid = units - 1 - 2 - 2
    return tuple(MXU_WIDTH * n for n in (1, 2, mid - mid // 2, mid // 2, 2))


def kernel(x, c, w_ada, b_ada, w_in, b_forget, w_fox_proj, w_sb_proj, w_o, ln1_g, ln1_b,
           w_up, conv_w, conv_b, w_down, ln2_g, ln2_b):
    bsz, s, d = x.shape
    d_ff = w_down.shape[0]
    tl = _tiles(s)
    aw = ATTN_WIDTH

    qscale = jnp.concatenate([jnp.full((aw,), ATTN_SCALE, F32), jnp.ones((2 * aw,), F32)])
    wa = (w_in[:, 0:3 * aw] * qscale).astype(BF16)
    wf = jnp.pad(w_in[:, 3 * aw:3 * aw + N_HEADS], ((0, 0), (0, LANES - N_HEADS))).astype(BF16)
    b0 = 3 * aw + N_HEADS
    wb = (w_in[:, b0:b0 + 3 * aw] * qscale).astype(BF16)
    wg = w_in[:, b0 + 3 * aw:].astype(BF16)

    for _ in range(DEPTH):
        mod = _ada(c, w_ada, b_ada).reshape(bsz, N_MOD, d)
        b_f = jnp.pad(b_forget, (0, LANES - N_HEADS)).reshape(1, LANES)
        qx, kx, vt, qn2, kn2, clast, qkvb = _inproj(x, mod, wa, wb, wf, b_f, tl["proj_rows"])
        tq, tk = tl["fox_queries"], tl["fox_keys"]
        assert tk == tl["proj_rows"] and tq % tk == 0
        qn2 = jnp.max(qn2[..., 0].reshape(bsz, s // tq, tq // tk, N_HEADS), axis=2)
        kn2 = jnp.max(kn2[..., 0], axis=1)
        qkmax = jnp.sqrt(jnp.transpose(qn2, (0, 2, 1)) * kn2[:, :, None])
        clast = jnp.transpose(clast[:, :, 0, :N_HEADS], (0, 2, 1))
        y_fox = _fox_attention(qx, kx, vt, clast.reshape(-1), qkmax.reshape(-1), tq, tk)
        y_fox = y_fox.reshape(bsz, ATTN_WIDTH, s)
        y_sb = _sb_attention(qkvb, tl["sb_block"], tl["sb_heads"], tl["sb_window"], tl["sb_rows"])
        x = _outproj(x, mod, y_fox, y_sb, wg, w_fox_proj.astype(BF16), w_sb_proj.astype(BF16),
                     w_o.astype(BF16), ln1_g.reshape(1, d), ln1_b.reshape(1, d), tl["proj_rows"])
        x = _ffn(x, mod, w_up.astype(BF16), conv_w, conv_b.reshape(1, 2 * d_ff), w_down.astype(BF16),
                 ln2_g.reshape(1, d), ln2_b.reshape(1, d), tl["ffn_rows"], _ffn_chunks(d_ff))
    return x
```

```python
import functools

import jax
import jax.numpy as jnp
from jax import lax
from jax.experimental import pallas as pl
from jax.experimental.pallas import tpu as pltpu

F32 = jnp.float32
BF16 = jnp.bfloat16

HEAD_DIM = 64
N_HEADS = 8
ATTN_WIDTH = N_HEADS * HEAD_DIM
N_MOD = 6
CONV_WIDTH = 3
LN_EPS = 1e-5
DEPTH = 1
DEEPNORM_ALPHA = (2.0 * DEPTH) ** 0.25
ATTN_SCALE = HEAD_DIM ** -0.5

LANES = 128
SUBLANES = 8
HEADS_PER_VREG_ROW = LANES // HEAD_DIM
VMEM_LIMIT = 56 * 1024 * 1024

EXP_ZERO_BELOW = -110.0


def _cparams(n_axes):
    return pltpu.CompilerParams(
        dimension_semantics=("arbitrary",) * n_axes, vmem_limit_bytes=VMEM_LIMIT)


def _resident(shape):
    nd = len(shape)
    return pl.BlockSpec(shape, lambda *_: (0,) * nd, pipeline_mode=pl.Buffered(1))


def _log_sigmoid(x):
    return jnp.minimum(x, 0.0) - jnp.log1p(jnp.exp(-jnp.abs(x)))


def _layer_norm(y, g, b):
    mu = jnp.mean(y, axis=-1, keepdims=True)
    d = y - mu
    var = jnp.mean(d * d, axis=-1, keepdims=True)
    return d * lax.rsqrt(var + LN_EPS) * g + b


def _ada_kernel(c_ref, w_ref, b_ref, o_ref):
    o_ref[...] = jnp.dot(c_ref[...], w_ref[...], preferred_element_type=F32,
                         precision=lax.Precision.HIGHEST) + b_ref[...]


def _ada(c, w_ada, b_ada):
    bsz, d = c.shape
    n = w_ada.shape[1]
    tn = n // 4
    return pl.pallas_call(
        _ada_kernel,
        grid=(n // tn,),
        in_specs=[pl.BlockSpec((bsz, d), lambda j: (0, 0)),
                  pl.BlockSpec((d, tn), lambda j: (0, j)),
                  pl.BlockSpec((1, tn), lambda j: (0, j))],
        out_specs=pl.BlockSpec((bsz, tn), lambda j: (0, j)),
        out_shape=jax.ShapeDtypeStruct((bsz, n), F32),
        compiler_params=_cparams(1),
        name="adaln_mod",
    )(c, w_ada, b_ada.reshape(1, n))


N_DECAY_TERMS = 3


def _split3(x):
    hi = x.astype(BF16)
    r = x - hi.astype(F32)
    mid = r.astype(BF16)
    lo = (r - mid.astype(F32)).astype(BF16)
    return hi, mid, lo


def _inproj_kernel(x_ref, mod_ref, wa_ref, wb_ref, wf_ref, bf_ref, tril_ref,
                   qx_ref, kx_ref, vt_ref, qn_ref, kn_ref, clast_ref, qkvb_ref, carry_ref):
    i = pl.program_id(1)
    tm = x_ref.shape[1]

    @pl.when(i == 0)
    def _():
        carry_ref[...] = jnp.zeros(carry_ref.shape, F32)

    x = x_ref[0]
    sh = mod_ref[0, 0:1, :]
    sc = mod_ref[0, 1:2, :]
    u = (x * (1.0 + sc) + sh).astype(BF16)
    f = jnp.dot(u, wf_ref[...], preferred_element_type=F32)
    lane = lax.broadcasted_iota(jnp.int32, (tm, LANES), 1)
    logf = jnp.where(lane < N_HEADS, _log_sigmoid(f + bf_ref[...]), 0.0)
    parts = jnp.dot(tril_ref[...], jnp.concatenate(_split3(logf), axis=1),
                    preferred_element_type=F32)
    cum = (parts[:, :LANES] + parts[:, LANES:2 * LANES] + parts[:, 2 * LANES:]) + carry_ref[...]
    carry_ref[...] = cum[tm - 1:tm, :]
    clast_ref[0, 0] = cum[tm - 1:tm, :]
    terms = [t.astype(F32) for t in _split3(-cum)]
    qkva = jnp.dot(u, wa_ref[...], preferred_element_type=F32).astype(BF16)
    qkvb_ref[0] = jnp.dot(u, wb_ref[...], preferred_element_type=F32).astype(BF16)

    def max_sq_norm(v):
        rows = jnp.sum(v * v, axis=1, keepdims=True)
        return jnp.broadcast_to(jnp.max(rows, axis=0, keepdims=True), (1, LANES))

    n_pairs = N_HEADS // HEADS_PER_VREG_ROW
    qns, kns = [], []
    for p in range(n_pairs):
        qp = qkva[:, p * LANES:(p + 1) * LANES].astype(F32)
        kp = qkva[:, (n_pairs + p) * LANES:(n_pairs + p + 1) * LANES].astype(F32)
        vp = qkva[:, (2 * n_pairs + p) * LANES:(2 * n_pairs + p + 1) * LANES].astype(F32)
        vt = jnp.transpose(vp).astype(BF16)
        for r in range(HEADS_PER_VREG_ROW):
            h = HEADS_PER_VREG_ROW * p + r
            own = (lane >= r * HEAD_DIM) & (lane < (r + 1) * HEAD_DIM)
            spare0 = (1 - r) * HEAD_DIM
            qx = jnp.where(own, qp, 0.0)
            kx = jnp.where(own, kp, 0.0)
            qns.append(max_sq_norm(qx))
            kns.append(max_sq_norm(kx))
            for n in range(N_DECAY_TERMS):
                at = lane == spare0 + n
                qx = jnp.where(at, 1.0, qx)
                kx = jnp.where(at, terms[n][:, h:h + 1], kx)
            qx_ref[0, h] = qx.astype(BF16)
            kx_ref[0, h] = kx.astype(BF16)
            vt_ref[0, h] = vt[r * HEAD_DIM:(r + 1) * HEAD_DIM, :]
    qn_ref[0, 0] = jnp.concatenate(qns, axis=0)
    kn_ref[0, 0] = jnp.concatenate(kns, axis=0)


def _inproj(x, mod, wa, wb, wf, bfg, tm):
    bsz, s, d = x.shape
    nt = s // tm
    row = lax.broadcasted_iota(jnp.int32, (tm, tm), 0)
    col = lax.broadcasted_iota(jnp.int32, (tm, tm), 1)
    tril = (col <= row).astype(BF16)
    head_rows = pl.BlockSpec((1, N_HEADS, tm, LANES), lambda b, i: (b, 0, i, 0))
    per_tile = pl.BlockSpec((1, 1, N_HEADS, LANES), lambda b, i: (b, i, 0, 0))
    return pl.pallas_call(
        _inproj_kernel,
        grid=(bsz, nt),
        in_specs=[pl.BlockSpec((1, tm, d), lambda b, i: (b, i, 0)),
                  pl.BlockSpec((1, N_MOD, d), lambda b, i: (b, 0, 0)),
                  _resident(wa.shape), _resident(wb.shape), _resident(wf.shape),
                  _resident(bfg.shape), _resident(tril.shape)],
        out_specs=[head_rows, head_rows,
                   pl.BlockSpec((1, N_HEADS, HEAD_DIM, tm), lambda b, i: (b, 0, 0, i)),
                   per_tile, per_tile,
                   pl.BlockSpec((1, 1, 1, LANES), lambda b, i: (b, i, 0, 0)),
                   pl.BlockSpec((1, tm, wb.shape[1]), lambda b, i: (b, i, 0))],
        out_shape=[jax.ShapeDtypeStruct((bsz, N_HEADS, s, LANES), BF16),
                   jax.ShapeDtypeStruct((bsz, N_HEADS, s, LANES), BF16),
                   jax.ShapeDtypeStruct((bsz, N_HEADS, HEAD_DIM, s), BF16),
                   jax.ShapeDtypeStruct((bsz, nt, N_HEADS, LANES), F32),
                   jax.ShapeDtypeStruct((bsz, nt, N_HEADS, LANES), F32),
                   jax.ShapeDtypeStruct((bsz, nt, 1, LANES), F32),
                   jax.ShapeDtypeStruct((bsz, s, wb.shape[1]), BF16)],
        scratch_shapes=[pltpu.VMEM((1, LANES), F32)],
        compiler_params=_cparams(2),
        name="in_proj",
    )(x, mod, wa, wb, wf, bfg, tril)


def _fox_kernel(clast_ref, qkmax_ref, qx_ref, kx_ref, vt_ref, o_ref,
                m_ref, l_ref, acc_ref, st_ref, *, tq, tk):
    bh = pl.program_id(0) * pl.num_programs(1) + pl.program_id(1)
    n_diag = tq // tk
    n_qblk = qx_ref.shape[2] // tq
    n_kblk = n_qblk * n_diag

    def scores(jb, q0, c0):
        row0 = pl.multiple_of(jb * tk, tk)
        return lax.dot_general(kx_ref[0, 0, pl.ds(row0, tk), :],
                               qx_ref[0, 0, pl.ds(q0 + c0, tq - c0), :],
                               (((1,), (1,)), ((), ())), preferred_element_type=F32)

    def softmax_pv(jb, c0, diagonal, buf):
        st = st_ref[buf, :, c0:]
        if diagonal:
            key = lax.broadcasted_iota(jnp.int32, st.shape, 0)
            qry = lax.broadcasted_iota(jnp.int32, st.shape, 1)
            st = jnp.where(key <= qry, st, -jnp.inf)
        m_prev = m_ref[:, c0:]
        m_new = jnp.maximum(m_prev, jnp.max(st, axis=0, keepdims=True))
        alpha = jnp.exp(m_prev - m_new)
        pe = jnp.exp(st - m_new)
        l_ref[:, c0:] = alpha * l_ref[:, c0:] + jnp.sum(pe, axis=0, keepdims=True)
        row0 = pl.multiple_of(jb * tk, tk)
        vblk = vt_ref[0, 0, :, pl.ds(row0, tk)]
        acc_ref[:, c0:] = alpha * acc_ref[:, c0:] + jnp.dot(vblk, pe.astype(BF16),
                                                            preferred_element_type=F32)
        m_ref[:, c0:] = m_new

    def query_block(i, carry):
        q0 = pl.multiple_of(i * tq, tq)
        m_ref[...] = jnp.full(m_ref.shape, -jnp.inf, F32)
        l_ref[...] = jnp.zeros(l_ref.shape, F32)
        acc_ref[...] = jnp.zeros(acc_ref.shape, F32)

        first = i * n_diag
        buf = 0
        st_ref[buf, :, (n_diag - 1) * tk:] = scores(first + n_diag - 1, q0, (n_diag - 1) * tk)
        for d in range(n_diag - 1, 0, -1):
            st_ref[1 - buf, :, (d - 1) * tk:] = scores(first + d - 1, q0, (d - 1) * tk)
            softmax_pv(first + d, d * tk, True, buf)
            buf = 1 - buf
        st_ref[1 - buf] = scores(jnp.maximum(first - 1, 0), q0, 0)
        softmax_pv(first, 0, True, buf)
        buf = 1 - buf

        bound_qk = qkmax_ref[bh * n_qblk + i]
        m_min = jnp.min(m_ref[...])

        def wanted(j):
            reach = bound_qk - clast_ref[bh * n_kblk + jnp.maximum(j, 0)] - m_min
            return jnp.logical_and(j >= 0, reach >= EXP_ZERO_BELOW)

        n_more = (first - 1) - lax.while_loop(wanted, lambda j: j - 1, first - 1)

        def two_blocks(k, unused):
            j = first - 1 - 2 * k
            st_ref[1 - buf] = scores(jnp.maximum(j - 1, 0), q0, 0)
            softmax_pv(j, 0, False, buf)
            st_ref[buf] = scores(jnp.maximum(j - 2, 0), q0, 0)
            softmax_pv(j - 1, 0, False, 1 - buf)
            return unused

        lax.fori_loop(0, n_more // 2, two_blocks, 0)

        @pl.when(n_more % 2 == 1)
        def _():
            softmax_pv(first - n_more, 0, False, buf)

        o_ref[0, 0, :, pl.ds(q0, tq)] = (acc_ref[...] / l_ref[...]).astype(o_ref.dtype)
        return carry

    lax.fori_loop(0, n_qblk, query_block, 0)


def _fox_attention(qx, kx, vt, clast, qkmax, tq, tk):
    bsz, nh, s, _ = qx.shape
    grid_spec = pltpu.PrefetchScalarGridSpec(
        num_scalar_prefetch=2,
        grid=(bsz, nh),
        in_specs=[pl.BlockSpec((1, 1, s, LANES), lambda b, h, *_: (b, h, 0, 0)),
                  pl.BlockSpec((1, 1, s, LANES), lambda b, h, *_: (b, h, 0, 0)),
                  pl.BlockSpec((1, 1, HEAD_DIM, s), lambda b, h, *_: (b, h, 0, 0))],
        out_specs=pl.BlockSpec((1, 1, HEAD_DIM, s), lambda b, h, *_: (b, h, 0, 0)),
        scratch_shapes=[pltpu.VMEM((1, tq), F32),
                        pltpu.VMEM((1, tq), F32),
                        pltpu.VMEM((HEAD_DIM, tq), F32),
                        pltpu.VMEM((2, tk, tq), F32)])
    return pl.pallas_call(
        functools.partial(_fox_kernel, tq=tq, tk=tk),
        grid_spec=grid_spec,
        out_shape=jax.ShapeDtypeStruct((bsz, nh, HEAD_DIM, s), BF16),
        compiler_params=_cparams(2),
        name="fox_attention",
    )(clast, qkmax, qx, kx, vt)


def _sb_kernel(q_ref, k_ref, v_ref, u_ref, o_ref, acc_ref, carry_ref, *, t, heads, window):
    n_pairs = heads // HEADS_PER_VREG_ROW
    lane = lax.broadcasted_iota(jnp.int32, (t, LANES), 1)
    strict = (lax.broadcasted_iota(jnp.int32, (t, t), 1) < lax.broadcasted_iota(jnp.int32, (t, t), 0))

    def visit(qs, blocks):
        kbs, vbs = [], []
        for j, _ in blocks:
            col0 = pl.multiple_of(jnp.maximum(j, 0) * t, t)
            kbs.append(k_ref[0, pl.ds(col0, t), :])
            vbs.append(v_ref[0, pl.ds(col0, t), :])
        zs = [[None] * heads for _ in blocks]
        for w in range(len(blocks)):
            for h in range(heads):
                p, r = divmod(h, HEADS_PER_VREG_ROW)
                zs[w][h] = lax.dot_general(qs[p][r], kbs[w][:, p * LANES:(p + 1) * LANES],
                                           (((1,), (1,)), ((), ())), preferred_element_type=F32)
        reds = [[None] * heads for _ in blocks]
        for w, (_, keep) in enumerate(blocks):
            for h in range(heads):
                z = zs[w][h]
                sp = jnp.maximum(z, 0.0) + jnp.log(1.0 + jnp.exp(-jnp.abs(z)))
                if keep is not None:
                    sp = jnp.where(keep, sp, 0.0)
                hi = sp.astype(BF16)
                lo = (sp - hi.astype(F32)).astype(BF16)
                reds[w][h] = jnp.dot(jnp.concatenate([hi, lo], axis=1), u_ref[...],
                                     preferred_element_type=F32)
        worst = None
        for h in range(heads):
            p = h // HEADS_PER_VREG_ROW
            carry = carry_ref[h]
            total = None
            for w, (_, keep) in enumerate(blocks):
                a = jnp.exp(zs[w][h] + reds[w][h][:, :t] + carry)
                if keep is not None:
                    a = jnp.where(keep, a, 0.0)
                pv = jnp.dot(a.astype(BF16), vbs[w][:, p * LANES:(p + 1) * LANES],
                             preferred_element_type=F32)
                total = pv if total is None else total + pv
                carry = carry + reds[w][h][:, t:]
            acc_ref[h] += total
            carry_ref[h] = carry
            worst = carry if worst is None else jnp.maximum(worst, carry)
        return jnp.max(worst)

    n_sub = q_ref.shape[1] // t

    def query_block(sub, unused):
        i = pl.program_id(2) * n_sub + sub
        r0 = pl.multiple_of(sub * t, t)
        qs = []
        for p in range(n_pairs):
            q = q_ref[0, pl.ds(r0, t), p * LANES:(p + 1) * LANES]
            zero = jnp.zeros_like(q)
            qs.append((jnp.where(lane < HEAD_DIM, q, zero), jnp.where(lane >= HEAD_DIM, q, zero)))
        acc_ref[...] = jnp.zeros(acc_ref.shape, F32)
        carry_ref[...] = jnp.zeros(carry_ref.shape, F32)

        worst0 = visit(qs, [(i, strict)] + [(i - w, i - w >= 0) for w in range(1, window)])

        def cond(state):
            j, worst = state
            return jnp.logical_and(j >= 0, worst >= EXP_ZERO_BELOW)

        def body(state):
            j, _ = state
            return j - 1, visit(qs, [(j, None)])

        lax.while_loop(cond, body, (i - window, worst0))

        for p in range(n_pairs):
            o_ref[0, pl.ds(r0, t), p * LANES:(p + 1) * LANES] = jnp.where(
                lane < HEAD_DIM, acc_ref[HEADS_PER_VREG_ROW * p], acc_ref[HEADS_PER_VREG_ROW * p + 1]
            ).astype(o_ref.dtype)
        return unused

    lax.fori_loop(0, n_sub, query_block, 0)


def _sb_attention(qkv, t, heads, window, rows):
    bsz, s, _ = qkv.shape
    n_groups = N_HEADS // heads
    w = heads * HEAD_DIM
    j = lax.broadcasted_iota(jnp.int32, (2 * t, 2 * t), 0) % t
    c = lax.broadcasted_iota(jnp.int32, (2 * t, 2 * t), 1)
    u = -jnp.logical_or(c >= t, j >= c).astype(BF16)
    return pl.pallas_call(
        functools.partial(_sb_kernel, t=t, heads=heads, window=window),
        grid=(bsz, n_groups, s // rows),
        in_specs=[pl.BlockSpec((1, rows, w), lambda b, g, i: (b, i, g)),
                  pl.BlockSpec((1, s, w), lambda b, g, i: (b, 0, n_groups + g)),
                  pl.BlockSpec((1, s, w), lambda b, g, i: (b, 0, 2 * n_groups + g)),
                  pl.BlockSpec(u.shape, lambda b, g, i: (0, 0))],
        out_specs=pl.BlockSpec((1, rows, w), lambda b, g, i: (b, i, g)),
        out_shape=jax.ShapeDtypeStruct((bsz, s, ATTN_WIDTH), BF16),
        scratch_shapes=[pltpu.VMEM((heads, t, LANES), F32),
                        pltpu.VMEM((heads, t, LANES), F32)],
        compiler_params=_cparams(3),
        name="stickbreak_attention",
    )(qkv, qkv, qkv, u)


def _outproj_kernel(x_ref, mod_ref, yf_ref, ys_ref, wg_ref, wfp_ref, wsp_ref, wo_ref, g_ref, b_ref, o_ref):
    x = x_ref[0]
    d = x.shape[-1]
    sh = mod_ref[0, 0:1, :]
    sc = mod_ref[0, 1:2, :]
    gt = mod_ref[0, 2:3, :]
    u = (x * (1.0 + sc) + sh).astype(BF16)
    gates = jnp.dot(u, wg_ref[...], preferred_element_type=F32)
    pa = lax.dot_general(yf_ref[0], wfp_ref[...], (((0,), (0,)), ((), ())), preferred_element_type=F32)
    pb = jnp.dot(ys_ref[0], wsp_ref[...], preferred_element_type=F32)
    merged = jax.nn.sigmoid(gates[:, :d]) * pa + jax.nn.sigmoid(gates[:, d:]) * pb
    attn = jnp.dot(merged.astype(BF16), wo_ref[...], preferred_element_type=F32)
    y = DEEPNORM_ALPHA * x + (1.0 + gt) * attn
    o_ref[0] = _layer_norm(y, g_ref[...], b_ref[...])


def _outproj(x, mod, yf, ys, wg, wfp, wsp, wo, g, b, tm):
    bsz, s, d = x.shape
    aw = ys.shape[-1]
    return pl.pallas_call(
        _outproj_kernel,
        grid=(bsz, s // tm),
        in_specs=[pl.BlockSpec((1, tm, d), lambda b_, i: (b_, i, 0)),
                  pl.BlockSpec((1, N_MOD, d), lambda b_, i: (b_, 0, 0)),
                  pl.BlockSpec((1, aw, tm), lambda b_, i: (b_, 0, i)),
                  pl.BlockSpec((1, tm, aw), lambda b_, i: (b_, i, 0)),
                  _resident(wg.shape), _resident(wfp.shape), _resident(wsp.shape), _resident(wo.shape),
                  _resident(g.shape), _resident(b.shape)],
        out_specs=pl.BlockSpec((1, tm, d), lambda b_, i: (b_, i, 0)),
        out_shape=jax.ShapeDtypeStruct((bsz, s, d), F32),
        compiler_params=_cparams(2),
        name="merge_out_proj_ln",
    )(x, mod, yf, ys, wg, wfp, wsp, wo, g, b)


def _ffn_kernel(x_ref, mod_ref, wup_ref, cw_ref, cb_ref, wdn_ref, g_ref, b_ref, o_ref,
                tail_ref, hbuf_ref, *, tm, chunks):
    i = pl.program_id(1)
    d_ff = wdn_ref.shape[0]
    halo = SUBLANES

    @pl.when(i == 0)
    def _():
        tail_ref[...] = jnp.zeros(tail_ref.shape, F32)

    x = x_ref[0]
    sh = mod_ref[0, 3:4, :]
    sc = mod_ref[0, 4:5, :]
    gt = mod_ref[0, 5:6, :]
    u = (x * (1.0 + sc) + sh).astype(BF16)

    def up(col0, n):
        return jnp.dot(u, wup_ref[:, col0:col0 + n], preferred_element_type=F32)

    def conv(h, col0, n):
        hbuf_ref[0:halo, 0:n] = tail_ref[:, col0:col0 + n]
        hbuf_ref[halo:halo + tm, 0:n] = h
        tail_ref[:, col0:col0 + n] = h[tm - halo:tm, :]
        h1 = hbuf_ref[halo - 1:halo - 1 + tm, 0:n]
        h2 = hbuf_ref[halo - 2:halo - 2 + tm, 0:n]
        w = cw_ref[:, col0:col0 + n]
        out = cb_ref[:, col0:col0 + n] + h2 * w[0:1, :]
        out = out + h1 * w[1:2, :]
        return out + h * w[2:3, :]

    starts = [sum(chunks[:c]) for c in range(len(chunks))]
    ups = [None] * len(chunks)
    ups[0] = (up(starts[0], chunks[0]), up(d_ff + starts[0], chunks[0]))
    acc = jnp.zeros(x.shape, F32)
    for c, n in enumerate(chunks):
        if c + 1 < len(chunks):
            ups[c + 1] = (up(starts[c + 1], chunks[c + 1]), up(d_ff + starts[c + 1], chunks[c + 1]))
        hg = conv(ups[c][0], starts[c], n)
        hv = conv(ups[c][1], d_ff + starts[c], n)
        act = (hg * jax.nn.sigmoid(hg)) * hv
        acc = acc + jnp.dot(act.astype(BF16), wdn_ref[starts[c]:starts[c] + n, :],
                            preferred_element_type=F32)

    y = DEEPNORM_ALPHA * x + (1.0 + gt) * acc
    o_ref[0] = _layer_norm(y, g_ref[...], b_ref[...])


def _ffn(x, mod, wup, cw, cb, wdn, g, b, tm, chunks):
    bsz, s, d = x.shape
    f2 = wup.shape[1]
    assert sum(chunks) == wdn.shape[0]
    return pl.pallas_call(
        functools.partial(_ffn_kernel, tm=tm, chunks=chunks),
        grid=(bsz, s // tm),
        in_specs=[pl.BlockSpec((1, tm, d), lambda b_, i: (b_, i, 0)),
                  pl.BlockSpec((1, N_MOD, d), lambda b_, i: (b_, 0, 0)),
                  _resident(wup.shape), _resident(cw.shape), _resident(cb.shape), _resident(wdn.shape),
                  _resident(g.shape), _resident(b.shape)],
        out_specs=pl.BlockSpec((1, tm, d), lambda b_, i: (b_, i, 0)),
        out_shape=jax.ShapeDtypeStruct((bsz, s, d), F32),
        scratch_shapes=[pltpu.VMEM((SUBLANES, f2), F32),
                        pltpu.VMEM((SUBLANES + tm, max(chunks)), F32)],
        compiler_params=_cparams(2),
        name="conv_ffn_ln",
    )(x, mod, wup, cw, cb, wdn, g, b)


def _tiles(s):
    return dict(
        proj_rows=min(512, s),
        fox_queries=min(1024, s),
        fox_keys=min(512, s),
        sb_block=LANES,
        sb_heads=N_HEADS,
        sb_window=3,
        sb_rows=min(1024, s),
        ffn_rows=min(512, s),
    )


MXU_WIDTH = 256


def _ffn_chunks(d_ff):
    units = d_ff // MXU_WIDTH
    assert units * MXU_WIDTH == d_ff and units >= 5
    mid = units - 1 - 2 - 2
    return tuple(MXU_WIDTH * n for n in (1, 2, mid - mid // 2, mid // 2, 2))


def kernel(x, c, w_ada, b_ada, w_in, b_forget, w_fox_proj, w_sb_proj, w_o, ln1_g, ln1_b,
           w_up, conv_w, conv_b, w_down, ln2_g, ln2_b):
    bsz, s, d = x.shape
    d_ff = w_down.shape[0]
    tl = _tiles(s)
    aw = ATTN_WIDTH

    qscale = jnp.concatenate([jnp.full((aw,), ATTN_SCALE, F32), jnp.ones((2 * aw,), F32)])
    wa = (w_in[:, 0:3 * aw] * qscale).astype(BF16)
    wf = jnp.pad(w_in[:, 3 * aw:3 * aw + N_HEADS], ((0, 0), (0, LANES - N_HEADS))).astype(BF16)
    b0 = 3 * aw + N_HEADS
    wb = (w_in[:, b0:b0 + 3 * aw] * qscale).astype(BF16)
    wg = w_in[:, b0 + 3 * aw:].astype(BF16)

    for _ in range(DEPTH):
        mod = _ada(c, w_ada, b_ada).reshape(bsz, N_MOD, d)
        b_f = jnp.pad(b_forget, (0, LANES - N_HEADS)).reshape(1, LANES)
        qx, kx, vt, qn2, kn2, clast, qkvb = _inproj(x, mod, wa, wb, wf, b_f, tl["proj_rows"])
        tq, tk = tl["fox_queries"], tl["fox_keys"]
        assert tk == tl["proj_rows"] and tq % tk == 0
        qn2 = jnp.max(qn2[..., 0].reshape(bsz, s // tq, tq // tk, N_HEADS), axis=2)
        kn2 = jnp.max(kn2[..., 0], axis=1)
        qkmax = jnp.sqrt(jnp.transpose(qn2, (0, 2, 1)) * kn2[:, :, None])
        clast = jnp.transpose(clast[:, :, 0, :N_HEADS], (0, 2, 1))
        y_fox = _fox_attention(qx, kx, vt, clast.reshape(-1), qkmax.reshape(-1), tq, tk)
        y_fox = y_fox.reshape(bsz, ATTN_WIDTH, s)
        y_sb = _sb_attention(qkvb, tl["sb_block"], tl["sb_heads"], tl["sb_window"], tl["sb_rows"])
        x = _outproj(x, mod, y_fox, y_sb, wg, w_fox_proj.astype(BF16), w_sb_proj.astype(BF16),
                     w_o.astype(BF16), ln1_g.reshape(1, d), ln1_b.reshape(1, d), tl["proj_rows"])
        x = _ffn(x, mod, w_up.astype(BF16), conv_w, conv_b.reshape(1, 2 * d_ff), w_down.astype(BF16),
                 ln2_g.reshape(1, d), ln2_b.reshape(1, d), tl["ffn_rows"], _ffn_chunks(d_ff))
    return x
```

```python
import functools

import jax
import jax.numpy as jnp
from jax import lax
from jax.experimental import pallas as pl
from jax.experimental.pallas import tpu as pltpu

F32 = jnp.float32
BF16 = jnp.bfloat16

HEAD_DIM = 64
N_HEADS = 8
ATTN_WIDTH = N_HEADS * HEAD_DIM
N_MOD = 6
CONV_WIDTH = 3
LN_EPS = 1e-5
DEPTH = 1
DEEPNORM_ALPHA = (2.0 * DEPTH) ** 0.25
ATTN_SCALE = HEAD_DIM ** -0.5

LANES = 128
SUBLANES = 8
MXU_WIDTH = 256
HEADS_PER_VREG_ROW = LANES // HEAD_DIM
VMEM_LIMIT = 56 * 1024 * 1024

EXP_ZERO_BELOW = -110.0
FIXED_MAX_REACH = 60.0


def _cparams(n_axes):
    return pltpu.CompilerParams(
        dimension_semantics=("arbitrary",) * n_axes, vmem_limit_bytes=VMEM_LIMIT)


def _resident(shape):
    nd = len(shape)
    return pl.BlockSpec(shape, lambda *_: (0,) * nd, pipeline_mode=pl.Buffered(1))


def _log_sigmoid(x):
    return jnp.minimum(x, 0.0) - jnp.log1p(jnp.exp(-jnp.abs(x)))


def _layer_norm(y, g, b):
    mu = jnp.mean(y, axis=-1, keepdims=True)
    d = y - mu
    var = jnp.mean(d * d, axis=-1, keepdims=True)
    return d * lax.rsqrt(var + LN_EPS) * g + b


def _ada_kernel(c_ref, w_ref, b_ref, o_ref):
    o_ref[...] = jnp.dot(c_ref[...], w_ref[...], preferred_element_type=F32,
                         precision=lax.Precision.HIGHEST) + b_ref[...]


def _ada(c, w_ada, b_ada):
    bsz, d = c.shape
    n = w_ada.shape[1]
    tn = n // 4
    return pl.pallas_call(
        _ada_kernel,
        grid=(n // tn,),
        in_specs=[pl.BlockSpec((bsz, d), lambda j: (0, 0)),
                  pl.BlockSpec((d, tn), lambda j: (0, j)),
                  pl.BlockSpec((1, tn), lambda j: (0, j))],
        out_specs=pl.BlockSpec((bsz, tn), lambda j: (0, j)),
        out_shape=jax.ShapeDtypeStruct((bsz, n), F32),
        compiler_params=_cparams(1),
        name="adaln_mod",
    )(c, w_ada, b_ada.reshape(1, n))


N_DECAY_TERMS = 3


def _split3(x):
    hi = x.astype(BF16)
    r = x - hi.astype(F32)
    mid = r.astype(BF16)
    lo = (r - mid.astype(F32)).astype(BF16)
    return hi, mid, lo


def _inproj_kernel(x_ref, mod_ref, wa_ref, wb_ref, wf_ref, bf_ref, tril_ref,
                   qx_ref, kx_ref, vt_ref, qn_ref, kn_ref, clast_ref, qkvb_ref, carry_ref):
    i = pl.program_id(1)
    tm = x_ref.shape[1]

    @pl.when(i == 0)
    def _():
        carry_ref[...] = jnp.zeros(carry_ref.shape, F32)

    x = x_ref[0]
    sh = mod_ref[0, 0:1, :]
    sc = mod_ref[0, 1:2, :]
    u = (x * (1.0 + sc) + sh).astype(BF16)
    f = jnp.dot(u, wf_ref[...], preferred_element_type=F32)
    lane = lax.broadcasted_iota(jnp.int32, (tm, LANES), 1)
    logf = jnp.where(lane < N_HEADS, _log_sigmoid(f + bf_ref[...]), 0.0)
    parts = jnp.dot(tril_ref[...], jnp.concatenate(_split3(logf), axis=1),
                    preferred_element_type=F32)
    cum = (parts[:, :LANES] + parts[:, LANES:2 * LANES] + parts[:, 2 * LANES:]) + carry_ref[...]
    carry_ref[...] = cum[tm - 1:tm, :]
    clast_ref[0, 0] = cum[tm - 1:tm, :]
    terms = [t.astype(F32) for t in _split3(-cum)]
    qkva = jnp.dot(u, wa_ref[...], preferred_element_type=F32).astype(BF16)
    qkvb_ref[0] = jnp.dot(u, wb_ref[...], preferred_element_type=F32).astype(BF16)

    def max_sq_norm(v):
        rows = jnp.sum(v * v, axis=1, keepdims=True)
        return jnp.broadcast_to(jnp.max(rows, axis=0, keepdims=True), (1, LANES))

    n_pairs = N_HEADS // HEADS_PER_VREG_ROW
    qns, kns = [], []
    for p in range(n_pairs):
        qp = qkva[:, p * LANES:(p + 1) * LANES].astype(F32)
        kp = qkva[:, (n_pairs + p) * LANES:(n_pairs + p + 1) * LANES].astype(F32)
        vp = qkva[:, (2 * n_pairs + p) * LANES:(2 * n_pairs + p + 1) * LANES].astype(F32)
        vt = jnp.transpose(vp).astype(BF16)
        for r in range(HEADS_PER_VREG_ROW):
            h = HEADS_PER_VREG_ROW * p + r
            own = (lane >= r * HEAD_DIM) & (lane < (r + 1) * HEAD_DIM)
            spare0 = (1 - r) * HEAD_DIM
            qx = jnp.where(own, qp, 0.0)
            kx = jnp.where(own, kp, 0.0)
            qns.append(max_sq_norm(qx))
            kns.append(max_sq_norm(kx))
            for n in range(N_DECAY_TERMS):
                at = lane == spare0 + n
                qx = jnp.where(at, 1.0, qx)
                kx = jnp.where(at, terms[n][:, h:h + 1], kx)
            qx_ref[0, h] = qx.astype(BF16)
            kx_ref[0, h] = kx.astype(BF16)
            vt_ref[0, h] = vt[r * HEAD_DIM:(r + 1) * HEAD_DIM, :]
    qn_ref[0, 0] = jnp.concatenate(qns, axis=0)
    kn_ref[0, 0] = jnp.concatenate(kns, axis=0)


def _inproj(x, mod, wa, wb, wf, bfg, tm):
    bsz, s, d = x.shape
    nt = s // tm
    row = lax.broadcasted_iota(jnp.int32, (tm, tm), 0)
    col = lax.broadcasted_iota(jnp.int32, (tm, tm), 1)
    tril = (col <= row).astype(BF16)
    head_rows = pl.BlockSpec((1, N_HEADS, tm, LANES), lambda b, i: (b, 0, i, 0))
    per_tile = pl.BlockSpec((1, 1, N_HEADS, LANES), lambda b, i: (b, i, 0, 0))
    return pl.pallas_call(
        _inproj_kernel,
        grid=(bsz, nt),
        in_specs=[pl.BlockSpec((1, tm, d), lambda b, i: (b, i, 0)),
                  pl.BlockSpec((1, N_MOD, d), lambda b, i: (b, 0, 0)),
                  _resident(wa.shape), _resident(wb.shape), _resident(wf.shape),
                  _resident(bfg.shape), _resident(tril.shape)],
        out_specs=[head_rows, head_rows,
                   pl.BlockSpec((1, N_HEADS, HEAD_DIM, tm), lambda b, i: (b, 0, 0, i)),
                   per_tile, per_tile,
                   pl.BlockSpec((1, 1, 1, LANES), lambda b, i: (b, i, 0, 0)),
                   pl.BlockSpec((1, tm, wb.shape[1]), lambda b, i: (b, i, 0))],
        out_shape=[jax.ShapeDtypeStruct((bsz, N_HEADS, s, LANES), BF16),
                   jax.ShapeDtypeStruct((bsz, N_HEADS, s, LANES), BF16),
                   jax.ShapeDtypeStruct((bsz, N_HEADS, HEAD_DIM, s), BF16),
                   jax.ShapeDtypeStruct((bsz, nt, N_HEADS, LANES), F32),
                   jax.ShapeDtypeStruct((bsz, nt, N_HEADS, LANES), F32),
                   jax.ShapeDtypeStruct((bsz, nt, 1, LANES), F32),
                   jax.ShapeDtypeStruct((bsz, s, wb.shape[1]), BF16)],
        scratch_shapes=[pltpu.VMEM((1, LANES), F32)],
        compiler_params=_cparams(2),
        name="in_proj",
    )(x, mod, wa, wb, wf, bfg, tril)


def _fox_kernel(clast_ref, qkmax_ref, qx_ref, kx_ref, vt_ref, o_ref,
                m_ref, l_ref, acc_ref, st_ref, pe_ref, *, tq, tk):
    bh = pl.program_id(0) * pl.num_programs(1) + pl.program_id(1)
    n_diag = tq // tk
    n_qblk = qx_ref.shape[2] // tq
    n_kblk = n_qblk * n_diag

    def scores(jb, q0, c0):
        row0 = pl.multiple_of(jb * tk, tk)
        return lax.dot_general(kx_ref[0, 0, pl.ds(row0, tk), :],
                               qx_ref[0, 0, pl.ds(q0 + c0, tq - c0), :],
                               (((1,), (1,)), ((), ())), preferred_element_type=F32)

    def softmax_pv(jb, c0, diagonal, buf):
        st = st_ref[buf, :, c0:]
        if diagonal:
            key = lax.broadcasted_iota(jnp.int32, st.shape, 0)
            qry = lax.broadcasted_iota(jnp.int32, st.shape, 1)
            st = jnp.where(key <= qry, st, -jnp.inf)
        m_prev = m_ref[:, c0:]
        m_new = jnp.maximum(m_prev, jnp.max(st, axis=0, keepdims=True))
        alpha = jnp.exp(m_prev - m_new)
        pe = jnp.exp(st - m_new)
        l_ref[:, c0:] = alpha * l_ref[:, c0:] + jnp.sum(pe, axis=0, keepdims=True)
        row0 = pl.multiple_of(jb * tk, tk)
        vblk = vt_ref[0, 0, :, pl.ds(row0, tk)]
        acc_ref[:, c0:] = alpha * acc_ref[:, c0:] + jnp.dot(vblk, pe.astype(BF16),
                                                            preferred_element_type=F32)
        m_ref[:, c0:] = m_new

    def query_block(i, carry):
        q0 = pl.multiple_of(i * tq, tq)
        m_ref[...] = jnp.full(m_ref.shape, -jnp.inf, F32)
        l_ref[...] = jnp.zeros(l_ref.shape, F32)
        acc_ref[...] = jnp.zeros(acc_ref.shape, F32)

        first = i * n_diag
        buf = 0
        st_ref[buf, :, (n_diag - 1) * tk:] = scores(first + n_diag - 1, q0, (n_diag - 1) * tk)
        for d in range(n_diag - 1, 0, -1):
            st_ref[1 - buf, :, (d - 1) * tk:] = scores(first + d - 1, q0, (d - 1) * tk)
            softmax_pv(first + d, d * tk, True, buf)
            buf = 1 - buf
        st_ref[1 - buf] = scores(jnp.maximum(first - 1, 0), q0, 0)
        softmax_pv(first, 0, True, buf)
        buf = 1 - buf

        bound_qk = qkmax_ref[bh * n_qblk + i]
        m_min = jnp.min(m_ref[...])

        def wanted(j):
            reach = bound_qk - clast_ref[bh * n_kblk + jnp.maximum(j, 0)] - m_min
            return jnp.logical_and(j >= 0, reach >= EXP_ZERO_BELOW)

        n_more = (first - 1) - lax.while_loop(wanted, lambda j: j - 1, first - 1)

        fixed_max = 2.0 * bound_qk <= FIXED_MAX_REACH

        def probs(s):
            pe = jnp.exp(s - m_ref[...])
            l_ref[...] += jnp.sum(pe, axis=0, keepdims=True)
            return pe.astype(BF16)

        def weighted_values(jb, pbuf):
            row0 = pl.multiple_of(jb * tk, tk)
            acc_ref[...] += jnp.dot(vt_ref[0, 0, :, pl.ds(row0, tk)], pe_ref[pbuf],
                                    preferred_element_type=F32)

        @pl.when(jnp.logical_and(fixed_max, n_more >= 1))
        def _():
            pe_ref[0] = probs(st_ref[buf])
            rest = n_more - 1

            def two_blocks(k, unused):
                j = first - 2 - 2 * k
                s = scores(j, q0, 0)
                weighted_values(j + 1, 0)
                pe_ref[1] = probs(s)
                s = scores(j - 1, q0, 0)
                weighted_values(j, 1)
                pe_ref[0] = probs(s)
                return unused

            lax.fori_loop(0, rest // 2, two_blocks, 0)
            j_next = first - 2 - 2 * (rest // 2)

            @pl.when(rest % 2 == 1)
            def _():
                s = scores(j_next, q0, 0)
                weighted_values(j_next + 1, 0)
                pe_ref[1] = probs(s)
                weighted_values(j_next, 1)

            @pl.when(rest % 2 == 0)
            def _():
                weighted_values(j_next + 1, 0)

        @pl.when(jnp.logical_not(fixed_max))
        def _():
            def two_blocks(k, unused):
                j = first - 1 - 2 * k
                st_ref[1 - buf] = scores(jnp.maximum(j - 1, 0), q0, 0)
                softmax_pv(j, 0, False, buf)
                st_ref[buf] = scores(jnp.maximum(j - 2, 0), q0, 0)
                softmax_pv(j - 1, 0, False, 1 - buf)
                return unused

            lax.fori_loop(0, n_more // 2, two_blocks, 0)

            @pl.when(n_more % 2 == 1)
            def _():
                softmax_pv(first - n_more, 0, False, buf)

        o_ref[0, 0, :, pl.ds(q0, tq)] = (acc_ref[...] / l_ref[...]).astype(o_ref.dtype)
        return carry

    lax.fori_loop(0, n_qblk, query_block, 0)


def _fox_attention(qx, kx, vt, clast, qkmax, tq, tk):
    bsz, nh, s, _ = qx.shape
    grid_spec = pltpu.PrefetchScalarGridSpec(
        num_scalar_prefetch=2,
        grid=(bsz, nh),
        in_specs=[pl.BlockSpec((1, 1, s, LANES), lambda b, h, *_: (b, h, 0, 0)),
                  pl.BlockSpec((1, 1, s, LANES), lambda b, h, *_: (b, h, 0, 0)),
                  pl.BlockSpec((1, 1, HEAD_DIM, s), lambda b, h, *_: (b, h, 0, 0))],
        out_specs=pl.BlockSpec((1, 1, HEAD_DIM, s), lambda b, h, *_: (b, h, 0, 0)),
        scratch_shapes=[pltpu.VMEM((1, tq), F32),
                        pltpu.VMEM((1, tq), F32),
                        pltpu.VMEM((HEAD_DIM, tq), F32),
                        pltpu.VMEM((2, tk, tq), F32),
                        pltpu.VMEM((2, tk, tq), BF16)])
    return pl.pallas_call(
        functools.partial(_fox_kernel, tq=tq, tk=tk),
        grid_spec=grid_spec,
        out_shape=jax.ShapeDtypeStruct((bsz, nh, HEAD_DIM, s), BF16),
        compiler_params=_cparams(2),
        name="fox_attention",
    )(clast, qkmax, qx, kx, vt)


def _sb_kernel(q_ref, k_ref, v_ref, u_ref, o_ref, acc_ref, carry_ref, *, t, heads, window):
    n_pairs = heads // HEADS_PER_VREG_ROW
    lane = lax.broadcasted_iota(jnp.int32, (t, LANES), 1)
    strict = (lax.broadcasted_iota(jnp.int32, (t, t), 1) < lax.broadcasted_iota(jnp.int32, (t, t), 0))

    def visit(qs, blocks):
        kbs, vbs = [], []
        for j, _ in blocks:
            col0 = pl.multiple_of(jnp.maximum(j, 0) * t, t)
            kbs.append(k_ref[0, pl.ds(col0, t), :])
            vbs.append(v_ref[0, pl.ds(col0, t), :])
        zs = [[None] * heads for _ in blocks]
        for w in range(len(blocks)):
            for h in range(heads):
                p, r = divmod(h, HEADS_PER_VREG_ROW)
                zs[w][h] = lax.dot_general(qs[p][r], kbs[w][:, p * LANES:(p + 1) * LANES],
                                           (((1,), (1,)), ((), ())), preferred_element_type=F32)
        reds = [[None] * heads for _ in blocks]
        for w, (_, keep) in enumerate(blocks):
            for h in range(heads):
                z = zs[w][h]
                sp = jnp.maximum(z, 0.0) + jnp.log(1.0 + jnp.exp(-jnp.abs(z)))
                if keep is not None:
                    sp = jnp.where(keep, sp, 0.0)
                hi = sp.astype(BF16)
                lo = (sp - hi.astype(F32)).astype(BF16)
                reds[w][h] = jnp.dot(jnp.concatenate([hi, lo], axis=1), u_ref[...],
                                     preferred_element_type=F32)
        worst = None
        for h in range(heads):
            p = h // HEADS_PER_VREG_ROW
            carry = carry_ref[h]
            total = None
            for w, (_, keep) in enumerate(blocks):
                a = jnp.exp(zs[w][h] + reds[w][h][:, :t] + carry)
                if keep is not None:
                    a = jnp.where(keep, a, 0.0)
                pv = jnp.dot(a.astype(BF16), vbs[w][:, p * LANES:(p + 1) * LANES],
                             preferred_element_type=F32)
                total = pv if total is None else total + pv
                carry = carry + reds[w][h][:, t:]
            acc_ref[h] += total
            carry_ref[h] = carry
            worst = carry if worst is None else jnp.maximum(worst, carry)
        return jnp.max(worst)

    n_sub = q_ref.shape[1] // t

    def query_block(sub, unused):
        i = pl.program_id(2) * n_sub + sub
        r0 = pl.multiple_of(sub * t, t)
        qs = []
        for p in range(n_pairs):
            q = q_ref[0, pl.ds(r0, t), p * LANES:(p + 1) * LANES]
            zero = jnp.zeros_like(q)
            qs.append((jnp.where(lane < HEAD_DIM, q, zero), jnp.where(lane >= HEAD_DIM, q, zero)))
        acc_ref[...] = jnp.zeros(acc_ref.shape, F32)
        carry_ref[...] = jnp.zeros(carry_ref.shape, F32)

        worst0 = visit(qs, [(i, strict)] + [(i - w, i - w >= 0) for w in range(1, window)])

        def cond(state):
            j, worst = state
            return jnp.logical_and(j >= 0, worst >= EXP_ZERO_BELOW)

        def body(state):
            j, _ = state
            return j - 1, visit(qs, [(j, None)])

        lax.while_loop(cond, body, (i - window, worst0))

        for p in range(n_pairs):
            o_ref[0, pl.ds(r0, t), p * LANES:(p + 1) * LANES] = jnp.where(
                lane < HEAD_DIM, acc_ref[HEADS_PER_VREG_ROW * p], acc_ref[HEADS_PER_VREG_ROW * p + 1]
            ).astype(o_ref.dtype)
        return unused

    lax.fori_loop(0, n_sub, query_block, 0)


def _sb_attention(qkv, t, heads, window, rows):
    bsz, s, _ = qkv.shape
    n_groups = N_HEADS // heads
    w = heads * HEAD_DIM
    j = lax.broadcasted_iota(jnp.int32, (2 * t, 2 * t), 0) % t
    c = lax.broadcasted_iota(jnp.int32, (2 * t, 2 * t), 1)
    u = -jnp.logical_or(c >= t, j >= c).astype(BF16)
    return pl.pallas_call(
        functools.partial(_sb_kernel, t=t, heads=heads, window=window),
        grid=(bsz, n_groups, s // rows),
        in_specs=[pl.BlockSpec((1, rows, w), lambda b, g, i: (b, i, g)),
                  pl.BlockSpec((1, s, w), lambda b, g, i: (b, 0, n_groups + g)),
                  pl.BlockSpec((1, s, w), lambda b, g, i: (b, 0, 2 * n_groups + g)),
                  pl.BlockSpec(u.shape, lambda b, g, i: (0, 0))],
        out_specs=pl.BlockSpec((1, rows, w), lambda b, g, i: (b, i, g)),
        out_shape=jax.ShapeDtypeStruct((bsz, s, ATTN_WIDTH), BF16),
        scratch_shapes=[pltpu.VMEM((heads, t, LANES), F32),
                        pltpu.VMEM((heads, t, LANES), F32)],
        compiler_params=_cparams(3),
        name="stickbreak_attention",
    )(qkv, qkv, qkv, u)


def _outproj_kernel(x_ref, mod_ref, yf_ref, ys_ref, wg_ref, wfp_ref, wsp_ref, wo_ref, g_ref, b_ref, o_ref):
    x = x_ref[0]
    d = x.shape[-1]
    sh = mod_ref[0, 0:1, :]
    sc = mod_ref[0, 1:2, :]
    gt = mod_ref[0, 2:3, :]
    u = (x * (1.0 + sc) + sh).astype(BF16)
    gates = jnp.dot(u, wg_ref[...], preferred_element_type=F32)
    pa = lax.dot_general(yf_ref[0], wfp_ref[...], (((0,), (0,)), ((), ())), preferred_element_type=F32)
    pb = jnp.dot(ys_ref[0], wsp_ref[...], preferred_element_type=F32)
    merged = jax.nn.sigmoid(gates[:, :d]) * pa + jax.nn.sigmoid(gates[:, d:]) * pb
    attn = jnp.dot(merged.astype(BF16), wo_ref[...], preferred_element_type=F32)
    y = DEEPNORM_ALPHA * x + (1.0 + gt) * attn
    o_ref[0] = _layer_norm(y, g_ref[...], b_ref[...])


def _outproj(x, mod, yf, ys, wg, wfp, wsp, wo, g, b, tm):
    bsz, s, d = x.shape
    aw = ys.shape[-1]
    return pl.pallas_call(
        _outproj_kernel,
        grid=(bsz, s // tm),
        in_specs=[pl.BlockSpec((1, tm, d), lambda b_, i: (b_, i, 0)),
                  pl.BlockSpec((1, N_MOD, d), lambda b_, i: (b_, 0, 0)),
                  pl.BlockSpec((1, aw, tm), lambda b_, i: (b_, 0, i)),
                  pl.BlockSpec((1, tm, aw), lambda b_, i: (b_, i, 0)),
                  _resident(wg.shape), _resident(wfp.shape), _resident(wsp.shape), _resident(wo.shape),
                  _resident(g.shape), _resident(b.shape)],
        out_specs=pl.BlockSpec((1, tm, d), lambda b_, i: (b_, i, 0)),
        out_shape=jax.ShapeDtypeStruct((bsz, s, d), F32),
        compiler_params=_cparams(2),
        name="merge_out_proj_ln",
    )(x, mod, yf, ys, wg, wfp, wsp, wo, g, b)


def _ffn_kernel(x_ref, mod_ref, wup_ref, cw_ref, cb_ref, wdn_ref, g_ref, b_ref, o_ref,
                tail_ref, hbuf_ref, *, tm, chunks):
    i = pl.program_id(1)
    d_ff = wdn_ref.shape[0]
    halo = SUBLANES

    @pl.when(i == 0)
    def _():
        tail_ref[...] = jnp.zeros(tail_ref.shape, F32)

    x = x_ref[0]
    sh = mod_ref[0, 3:4, :]
    sc = mod_ref[0, 4:5, :]
    gt = mod_ref[0, 5:6, :]
    u = (x * (1.0 + sc) + sh).astype(BF16)

    def up(col0, n):
        return jnp.dot(u, wup_ref[:, col0:col0 + n], preferred_element_type=F32)

    def conv(h, col0, n):
        hbuf_ref[0:halo, 0:n] = tail_ref[:, col0:col0 + n]
        hbuf_ref[halo:halo + tm, 0:n] = h
        tail_ref[:, col0:col0 + n] = h[tm - halo:tm, :]
        h1 = hbuf_ref[halo - 1:halo - 1 + tm, 0:n]
        h2 = hbuf_ref[halo - 2:halo - 2 + tm, 0:n]
        w = cw_ref[:, col0:col0 + n]
        out = cb_ref[:, col0:col0 + n] + h2 * w[0:1, :]
        out = out + h1 * w[1:2, :]
        return out + h * w[2:3, :]

    starts = [sum(chunks[:c]) for c in range(len(chunks))]
    ups = [None] * len(chunks)
    ups[0] = (up(starts[0], chunks[0]), up(d_ff + starts[0], chunks[0]))
    acc = jnp.zeros(x.shape, F32)
    for c, n in enumerate(chunks):
        if c + 1 < len(chunks):
            ups[c + 1] = (up(starts[c + 1], chunks[c + 1]), up(d_ff + starts[c + 1], chunks[c + 1]))
        hg = conv(ups[c][0], starts[c], n)
        hv = conv(ups[c][1], d_ff + starts[c], n)
        act = (hg * jax.nn.sigmoid(hg)) * hv
        acc = acc + jnp.dot(act.astype(BF16), wdn_ref[starts[c]:starts[c] + n, :],
                            preferred_element_type=F32)

    y = DEEPNORM_ALPHA * x + (1.0 + gt) * acc
    o_ref[0] = _layer_norm(y, g_ref[...], b_ref[...])


def _ffn(x, mod, wup, cw, cb, wdn, g, b, tm, chunks):
    bsz, s, d = x.shape
    f2 = wup.shape[1]
    assert sum(chunks) == wdn.shape[0] and cw.shape[0] == CONV_WIDTH and CONV_WIDTH - 1 <= SUBLANES
    return pl.pallas_call(
        functools.partial(_ffn_kernel, tm=tm, chunks=chunks),
        grid=(bsz, s // tm),
        in_specs=[pl.BlockSpec((1, tm, d), lambda b_, i: (b_, i, 0)),
                  pl.BlockSpec((1, N_MOD, d), lambda b_, i: (b_, 0, 0)),
                  _resident(wup.shape), _resident(cw.shape), _resident(cb.shape), _resident(wdn.shape),
                  _resident(g.shape), _resident(b.shape)],
        out_specs=pl.BlockSpec((1, tm, d), lambda b_, i: (b_, i, 0)),
        out_shape=jax.ShapeDtypeStruct((bsz, s, d), F32),
        scratch_shapes=[pltpu.VMEM((SUBLANES, f2), F32),
                        pltpu.VMEM((SUBLANES + tm, max(chunks)), F32)],
        compiler_params=_cparams(2),
        name="conv_ffn_ln",
    )(x, mod, wup, cw, cb, wdn, g, b)


def _tiles(s):
    return dict(
        proj_rows=min(512, s),
        fox_queries=min(1024, s),
        fox_keys=min(512, s),
        sb_block=LANES,
        sb_heads=N_HEADS,
        sb_window=3,
        sb_rows=min(1024, s),
        ffn_rows=min(512, s),
    )


def _ffn_chunks(d_ff):
    units = d_ff // MXU_WIDTH
    assert units * MXU_WIDTH == d_ff and units >= 5
    mid = units - 1 - 2 - 2
    return tuple(MXU_WIDTH * n for n in (1, 2, mid - mid // 2, mid // 2, 2))


def kernel(x, c, w_ada, b_ada, w_in, b_forget, w_fox_proj, w_sb_proj, w_o, ln1_g, ln1_b,
           w_up, conv_w, conv_b, w_down, ln2_g, ln2_b):
    bsz, s, d = x.shape
    d_ff = w_down.shape[0]
    tl = _tiles(s)
    aw = ATTN_WIDTH

    qscale = jnp.concatenate([jnp.full((aw,), ATTN_SCALE, F32), jnp.ones((2 * aw,), F32)])
    wa = (w_in[:, 0:3 * aw] * qscale).astype(BF16)
    wf = jnp.pad(w_in[:, 3 * aw:3 * aw + N_HEADS], ((0, 0), (0, LANES - N_HEADS))).astype(BF16)
    b0 = 3 * aw + N_HEADS
    wb = (w_in[:, b0:b0 + 3 * aw] * qscale).astype(BF16)
    wg = w_in[:, b0 + 3 * aw:].astype(BF16)

    for _ in range(DEPTH):
        mod = _ada(c, w_ada, b_ada).reshape(bsz, N_MOD, d)
        b_f = jnp.pad(b_forget, (0, LANES - N_HEADS)).reshape(1, LANES)
        qx, kx, vt, qn2, kn2, clast, qkvb = _inproj(x, mod, wa, wb, wf, b_f, tl["proj_rows"])
        tq, tk = tl["fox_queries"], tl["fox_keys"]
        assert tk == tl["proj_rows"] and tq % tk == 0
        qn2 = jnp.max(qn2[..., 0].reshape(bsz, s // tq, tq // tk, N_HEADS), axis=2)
        kn2 = jnp.max(kn2[..., 0], axis=1)
        qkmax = jnp.sqrt(jnp.transpose(qn2, (0, 2, 1)) * kn2[:, :, None])
        clast = jnp.transpose(clast[:, :, 0, :N_HEADS], (0, 2, 1))
        y_fox = _fox_attention(qx, kx, vt, clast.reshape(-1), qkmax.reshape(-1), tq, tk)
        y_fox = y_fox.reshape(bsz, ATTN_WIDTH, s)
        y_sb = _sb_attention(qkvb, tl["sb_block"], tl["sb_heads"], tl["sb_window"], tl["sb_rows"])
        x = _outproj(x, mod, y_fox, y_sb, wg, w_fox_proj.astype(BF16), w_sb_proj.astype(BF16),
                     w_o.astype(BF16), ln1_g.reshape(1, d), ln1_b.reshape(1, d), tl["proj_rows"])
        x = _ffn(x, mod, w_up.astype(BF16), conv_w, conv_b.reshape(1, 2 * d_ff), w_down.astype(BF16),
                 ln2_g.reshape(1, d), ln2_b.reshape(1, d), tl["ffn_rows"], _ffn_chunks(d_ff))
    return x
```

```python
import functools

import jax
import jax.numpy as jnp
from jax import lax
from jax.experimental import pallas as pl
from jax.experimental.pallas import tpu as pltpu

F32 = jnp.float32
BF16 = jnp.bfloat16

HEAD_DIM = 64
N_HEADS = 8
ATTN_WIDTH = N_HEADS * HEAD_DIM
N_MOD = 6
CONV_WIDTH = 3
LN_EPS = 1e-5
DEPTH = 1
DEEPNORM_ALPHA = (2.0 * DEPTH) ** 0.25
ATTN_SCALE = HEAD_DIM ** -0.5

LANES = 128
SUBLANES = 8
MXU_WIDTH = 256
HEADS_PER_VREG_ROW = LANES // HEAD_DIM
VMEM_LIMIT = 56 * 1024 * 1024

EXP_ZERO_BELOW = -110.0
FIXED_MAX_REACH = 60.0


def _cparams(n_axes):
    return pltpu.CompilerParams(
        dimension_semantics=("arbitrary",) * n_axes, vmem_limit_bytes=VMEM_LIMIT)


def _resident(shape):
    nd = len(shape)
    return pl.BlockSpec(shape, lambda *_: (0,) * nd, pipeline_mode=pl.Buffered(1))


def _log_sigmoid(x):
    return jnp.minimum(x, 0.0) - jnp.log1p(jnp.exp(-jnp.abs(x)))


def _layer_norm(y, g, b):
    mu = jnp.mean(y, axis=-1, keepdims=True)
    d = y - mu
    var = jnp.mean(d * d, axis=-1, keepdims=True)
    return d * lax.rsqrt(var + LN_EPS) * g + b


def _ada_kernel(c_ref, w_ref, b_ref, o_ref):
    o_ref[...] = jnp.dot(c_ref[...], w_ref[...], preferred_element_type=F32,
                         precision=lax.Precision.HIGHEST) + b_ref[...]


def _ada(c, w_ada, b_ada):
    bsz, d = c.shape
    n = w_ada.shape[1]
    tn = n // 4
    return pl.pallas_call(
        _ada_kernel,
        grid=(n // tn,),
        in_specs=[pl.BlockSpec((bsz, d), lambda j: (0, 0)),
                  pl.BlockSpec((d, tn), lambda j: (0, j)),
                  pl.BlockSpec((1, tn), lambda j: (0, j))],
        out_specs=pl.BlockSpec((bsz, tn), lambda j: (0, j)),
        out_shape=jax.ShapeDtypeStruct((bsz, n), F32),
        compiler_params=_cparams(1),
        name="adaln_mod",
    )(c, w_ada, b_ada.reshape(1, n))


N_DECAY_TERMS = 3


def _split3(x):
    hi = x.astype(BF16)
    r = x - hi.astype(F32)
    mid = r.astype(BF16)
    lo = (r - mid.astype(F32)).astype(BF16)
    return hi, mid, lo


def _inproj_kernel(x_ref, mod_ref, wa_ref, wb_ref, wf_ref, bf_ref, tril_ref,
                   qx_ref, kx_ref, vt_ref, qn_ref, kn_ref, clast_ref, qkvb_ref, carry_ref):
    i = pl.program_id(1)
    tm = x_ref.shape[1]

    @pl.when(i == 0)
    def _():
        carry_ref[...] = jnp.zeros(carry_ref.shape, F32)

    x = x_ref[0]
    sh = mod_ref[0, 0:1, :]
    sc = mod_ref[0, 1:2, :]
    u = (x * (1.0 + sc) + sh).astype(BF16)
    f = jnp.dot(u, wf_ref[...], preferred_element_type=F32)
    lane = lax.broadcasted_iota(jnp.int32, (tm, LANES), 1)
    logf = jnp.where(lane < N_HEADS, _log_sigmoid(f + bf_ref[...]), 0.0)
    parts = jnp.dot(tril_ref[...], jnp.concatenate(_split3(logf), axis=1),
                    preferred_element_type=F32)
    cum = (parts[:, :LANES] + parts[:, LANES:2 * LANES] + parts[:, 2 * LANES:]) + carry_ref[...]
    carry_ref[...] = cum[tm - 1:tm, :]
    clast_ref[0, 0] = cum[tm - 1:tm, :]
    terms = [t.astype(F32) for t in _split3(-cum)]
    qkva = jnp.dot(u, wa_ref[...], preferred_element_type=F32).astype(BF16)
    qkvb_ref[0] = jnp.dot(u, wb_ref[...], preferred_element_type=F32).astype(BF16)

    def max_sq_norm(v):
        rows = jnp.sum(v * v, axis=1, keepdims=True)
        return jnp.broadcast_to(jnp.max(rows, axis=0, keepdims=True), (1, LANES))

    n_pairs = N_HEADS // HEADS_PER_VREG_ROW
    qns, kns = [], []
    for p in range(n_pairs):
        qp = qkva[:, p * LANES:(p + 1) * LANES].astype(F32)
        kp = qkva[:, (n_pairs + p) * LANES:(n_pairs + p + 1) * LANES].astype(F32)
        vp = qkva[:, (2 * n_pairs + p) * LANES:(2 * n_pairs + p + 1) * LANES].astype(F32)
        vt = jnp.transpose(vp).astype(BF16)
        for r in range(HEADS_PER_VREG_ROW):
            h = HEADS_PER_VREG_ROW * p + r
            own = (lane >= r * HEAD_DIM) & (lane < (r + 1) * HEAD_DIM)
            spare0 = (1 - r) * HEAD_DIM
            qx = jnp.where(own, qp, 0.0)
            kx = jnp.where(own, kp, 0.0)
            qns.append(max_sq_norm(qx))
            kns.append(max_sq_norm(kx))
            for n in range(N_DECAY_TERMS):
                at = lane == spare0 + n
                qx = jnp.where(at, 1.0, qx)
                kx = jnp.where(at, terms[n][:, h:h + 1], kx)
            qx_ref[0, h] = qx.astype(BF16)
            kx_ref[0, h] = kx.astype(BF16)
            vt_ref[0, h] = vt[r * HEAD_DIM:(r + 1) * HEAD_DIM, :]
    qn_ref[0, 0] = jnp.concatenate(qns, axis=0)
    kn_ref[0, 0] = jnp.concatenate(kns, axis=0)


def _inproj(x, mod, wa, wb, wf, bfg, tm):
    bsz, s, d = x.shape
    nt = s // tm
    row = lax.broadcasted_iota(jnp.int32, (tm, tm), 0)
    col = lax.broadcasted_iota(jnp.int32, (tm, tm), 1)
    tril = (col <= row).astype(BF16)
    head_rows = pl.BlockSpec((1, N_HEADS, tm, LANES), lambda b, i: (b, 0, i, 0))
    per_tile = pl.BlockSpec((1, 1, N_HEADS, LANES), lambda b, i: (b, i, 0, 0))
    return pl.pallas_call(
        _inproj_kernel,
        grid=(bsz, nt),
        in_specs=[pl.BlockSpec((1, tm, d), lambda b, i: (b, i, 0)),
                  pl.BlockSpec((1, N_MOD, d), lambda b, i: (b, 0, 0)),
                  _resident(wa.shape), _resident(wb.shape), _resident(wf.shape),
                  _resident(bfg.shape), _resident(tril.shape)],
        out_specs=[head_rows, head_rows,
                   pl.BlockSpec((1, N_HEADS, HEAD_DIM, tm), lambda b, i: (b, 0, 0, i)),
                   per_tile, per_tile,
                   pl.BlockSpec((1, 1, 1, LANES), lambda b, i: (b, i, 0, 0)),
                   pl.BlockSpec((1, tm, wb.shape[1]), lambda b, i: (b, i, 0))],
        out_shape=[jax.ShapeDtypeStruct((bsz, N_HEADS, s, LANES), BF16),
                   jax.ShapeDtypeStruct((bsz, N_HEADS, s, LANES), BF16),
                   jax.ShapeDtypeStruct((bsz, N_HEADS, HEAD_DIM, s), BF16),
                   jax.ShapeDtypeStruct((bsz, nt, N_HEADS, LANES), F32),
                   jax.ShapeDtypeStruct((bsz, nt, N_HEADS, LANES), F32),
                   jax.ShapeDtypeStruct((bsz, nt, 1, LANES), F32),
                   jax.ShapeDtypeStruct((bsz, s, wb.shape[1]), BF16)],
        scratch_shapes=[pltpu.VMEM((1, LANES), F32)],
        compiler_params=_cparams(2),
        name="in_proj",
    )(x, mod, wa, wb, wf, bfg, tril)


def _fox_kernel(clast_ref, qkmax_ref, qx_ref, kx_ref, vt_ref, o_ref,
                m_ref, l_ref, acc_ref, st_ref, pe_ref, *, tq, tk):
    bh = pl.program_id(0) * pl.num_programs(1) + pl.program_id(1)
    n_diag = tq // tk
    n_qblk = qx_ref.shape[2] // tq
    n_kblk = n_qblk * n_diag

    def scores(jb, q0, c0):
        row0 = pl.multiple_of(jb * tk, tk)
        return lax.dot_general(kx_ref[0, 0, pl.ds(row0, tk), :],
                               qx_ref[0, 0, pl.ds(q0 + c0, tq - c0), :],
                               (((1,), (1,)), ((), ())), preferred_element_type=F32)

    def softmax_pv(jb, c0, diagonal, buf):
        st = st_ref[buf, :, c0:]
        if diagonal:
            key = lax.broadcasted_iota(jnp.int32, st.shape, 0)
            qry = lax.broadcasted_iota(jnp.int32, st.shape, 1)
            st = jnp.where(key <= qry, st, -jnp.inf)
        m_prev = m_ref[:, c0:]
        m_new = jnp.maximum(m_prev, jnp.max(st, axis=0, keepdims=True))
        alpha = jnp.exp(m_prev - m_new)
        pe = jnp.exp(st - m_new)
        l_ref[:, c0:] = alpha * l_ref[:, c0:] + jnp.sum(pe, axis=0, keepdims=True)
        row0 = pl.multiple_of(jb * tk, tk)
        vblk = vt_ref[0, 0, :, pl.ds(row0, tk)]
        acc_ref[:, c0:] = alpha * acc_ref[:, c0:] + jnp.dot(vblk, pe.astype(BF16),
                                                            preferred_element_type=F32)
        m_ref[:, c0:] = m_new

    def query_block(i, carry):
        q0 = pl.multiple_of(i * tq, tq)
        m_ref[...] = jnp.full(m_ref.shape, -jnp.inf, F32)
        l_ref[...] = jnp.zeros(l_ref.shape, F32)
        acc_ref[...] = jnp.zeros(acc_ref.shape, F32)

        first = i * n_diag
        buf = 0
        st_ref[buf, :, (n_diag - 1) * tk:] = scores(first + n_diag - 1, q0, (n_diag - 1) * tk)
        for d in range(n_diag - 1, 0, -1):
            st_ref[1 - buf, :, (d - 1) * tk:] = scores(first + d - 1, q0, (d - 1) * tk)
            softmax_pv(first + d, d * tk, True, buf)
            buf = 1 - buf
        st_ref[1 - buf] = scores(jnp.maximum(first - 1, 0), q0, 0)
        softmax_pv(first, 0, True, buf)
        buf = 1 - buf

        bound_qk = qkmax_ref[bh * n_qblk + i]
        m_min = jnp.min(m_ref[...])

        def wanted(j):
            reach = bound_qk - clast_ref[bh * n_kblk + jnp.maximum(j, 0)] - m_min
            return jnp.logical_and(j >= 0, reach >= EXP_ZERO_BELOW)

        n_more = (first - 1) - lax.while_loop(wanted, lambda j: j - 1, first - 1)

        fixed_max = 2.0 * bound_qk <= FIXED_MAX_REACH

        def probs(s):
            pe = jnp.exp(s - m_ref[...])
            l_ref[...] += jnp.sum(pe, axis=0, keepdims=True)
            return pe.astype(BF16)

        def weighted_values(jb, pbuf):
            row0 = pl.multiple_of(jb * tk, tk)
            acc_ref[...] += jnp.dot(vt_ref[0, 0, :, pl.ds(row0, tk)], pe_ref[pbuf],
                                    preferred_element_type=F32)

        @pl.when(jnp.logical_and(fixed_max, n_more >= 1))
        def _():
            pe_ref[0] = probs(st_ref[buf])
            rest = n_more - 1

            def two_blocks(k, unused):
                j = first - 2 - 2 * k
                s = scores(j, q0, 0)
                weighted_values(j + 1, 0)
                pe_ref[1] = probs(s)
                s = scores(j - 1, q0, 0)
                weighted_values(j, 1)
                pe_ref[0] = probs(s)
                return unused

            lax.fori_loop(0, rest // 2, two_blocks, 0)
            j_next = first - 2 - 2 * (rest // 2)

            @pl.when(rest % 2 == 1)
            def _():
                s = scores(j_next, q0, 0)
                weighted_values(j_next + 1, 0)
                pe_ref[1] = probs(s)
                weighted_values(j_next, 1)

            @pl.when(rest % 2 == 0)
            def _():
                weighted_values(j_next + 1, 0)

        @pl.when(jnp.logical_not(fixed_max))
        def _():
            def two_blocks(k, unused):
                j = first - 1 - 2 * k
                st_ref[1 - buf] = scores(jnp.maximum(j - 1, 0), q0, 0)
                softmax_pv(j, 0, False, buf)
                st_ref[buf] = scores(jnp.maximum(j - 2, 0), q0, 0)
                softmax_pv(j - 1, 0, False, 1 - buf)
                return unused

            lax.fori_loop(0, n_more // 2, two_blocks, 0)

            @pl.when(n_more % 2 == 1)
            def _():
                softmax_pv(first - n_more, 0, False, buf)

        o_ref[0, 0, :, pl.ds(q0, tq)] = (acc_ref[...] / l_ref[...]).astype(o_ref.dtype)
        return carry

    lax.fori_loop(0, n_qblk, query_block, 0)


def _fox_attention(qx, kx, vt, clast, qkmax, tq, tk):
    bsz, nh, s, _ = qx.shape
    grid_spec = pltpu.PrefetchScalarGridSpec(
        num_scalar_prefetch=2,
        grid=(bsz, nh),
        in_specs=[pl.BlockSpec((1, 1, s, LANES), lambda b, h, *_: (b, h, 0, 0)),
                  pl.BlockSpec((1, 1, s, LANES), lambda b, h, *_: (b, h, 0, 0)),
                  pl.BlockSpec((1, 1, HEAD_DIM, s), lambda b, h, *_: (b, h, 0, 0))],
        out_specs=pl.BlockSpec((1, 1, HEAD_DIM, s), lambda b, h, *_: (b, h, 0, 0)),
        scratch_shapes=[pltpu.VMEM((1, tq), F32),
                        pltpu.VMEM((1, tq), F32),
                        pltpu.VMEM((HEAD_DIM, tq), F32),
                        pltpu.VMEM((2, tk, tq), F32),
                        pltpu.VMEM((2, tk, tq), BF16)])
    return pl.pallas_call(
        functools.partial(_fox_kernel, tq=tq, tk=tk),
        grid_spec=grid_spec,
        out_shape=jax.ShapeDtypeStruct((bsz, nh, HEAD_DIM, s), BF16),
        compiler_params=_cparams(2),
        name="fox_attention",
    )(clast, qkmax, qx, kx, vt)


def _sb_kernel(q_ref, k_ref, v_ref, u_ref, o_ref, acc_ref, carry_ref, *, t, heads, window):
    n_pairs = heads // HEADS_PER_VREG_ROW
    lane = lax.broadcasted_iota(jnp.int32, (t, LANES), 1)
    strict = (lax.broadcasted_iota(jnp.int32, (t, t), 1) < lax.broadcasted_iota(jnp.int32, (t, t), 0))

    def visit(qs, blocks):
        kbs, vbs = [], []
        for j, _ in blocks:
            col0 = pl.multiple_of(jnp.maximum(j, 0) * t, t)
            kbs.append(k_ref[0, pl.ds(col0, t), :])
            vbs.append(v_ref[0, pl.ds(col0, t), :])
        zs = [[None] * heads for _ in blocks]
        for w in range(len(blocks)):
            for h in range(heads):
                p, r = divmod(h, HEADS_PER_VREG_ROW)
                zs[w][h] = lax.dot_general(qs[p][r], kbs[w][:, p * LANES:(p + 1) * LANES],
                                           (((1,), (1,)), ((), ())), preferred_element_type=F32)
        reds = [[None] * heads for _ in blocks]
        for w, (_, keep) in enumerate(blocks):
            for h in range(heads):
                z = zs[w][h]
                sp = jnp.maximum(z, 0.0) + jnp.log(1.0 + jnp.exp(-jnp.abs(z)))
                if keep is not None:
                    sp = jnp.where(keep, sp, 0.0)
                hi = sp.astype(BF16)
                lo = (sp - hi.astype(F32)).astype(BF16)
                reds[w][h] = jnp.dot(jnp.concatenate([hi, lo], axis=1), u_ref[...],
                                     preferred_element_type=F32)
        worst = None
        for h in range(heads):
            p = h // HEADS_PER_VREG_ROW
            carry = carry_ref[h]
            total = None
            for w, (_, keep) in enumerate(blocks):
                a = jnp.exp(zs[w][h] + reds[w][h][:, :t] + carry)
                if keep is not None:
                    a = jnp.where(keep, a, 0.0)
                pv = jnp.dot(a.astype(BF16), vbs[w][:, p * LANES:(p + 1) * LANES],
                             preferred_element_type=F32)
                total = pv if total is None else total + pv
                carry = carry + reds[w][h][:, t:]
            acc_ref[h] += total
            carry_ref[h] = carry
            worst = carry if worst is None else jnp.maximum(worst, carry)
        return jnp.max(worst)

    n_sub = q_ref.shape[1] // t

    def query_block(sub, unused):
        i = pl.program_id(2) * n_sub + sub
        r0 = pl.multiple_of(sub * t, t)
        qs = []
        for p in range(n_pairs):
            q = q_ref[0, pl.ds(r0, t), p * LANES:(p + 1) * LANES]
            zero = jnp.zeros_like(q)
            qs.append((jnp.where(lane < HEAD_DIM, q, zero), jnp.where(lane >= HEAD_DIM, q, zero)))
        acc_ref[...] = jnp.zeros(acc_ref.shape, F32)
        carry_ref[...] = jnp.zeros(carry_ref.shape, F32)

        worst0 = visit(qs, [(i, strict)] + [(i - w, i - w >= 0) for w in range(1, window)])

        def cond(state):
            j, worst = state
            return jnp.logical_and(j >= 0, worst >= EXP_ZERO_BELOW)

        def body(state):
            j, _ = state
            return j - 1, visit(qs, [(j, None)])

        lax.while_loop(cond, body, (i - window, worst0))

        for p in range(n_pairs):
            o_ref[0, pl.ds(r0, t), p * LANES:(p + 1) * LANES] = jnp.where(
                lane < HEAD_DIM, acc_ref[HEADS_PER_VREG_ROW * p], acc_ref[HEADS_PER_VREG_ROW * p + 1]
            ).astype(o_ref.dtype)
        return unused

    lax.fori_loop(0, n_sub, query_block, 0)


def _sb_attention(qkv, t, heads, window, rows):
    bsz, s, _ = qkv.shape
    n_groups = N_HEADS // heads
    w = heads * HEAD_DIM
    j = lax.broadcasted_iota(jnp.int32, (2 * t, 2 * t), 0) % t
    c = lax.broadcasted_iota(jnp.int32, (2 * t, 2 * t), 1)
    u = -jnp.logical_or(c >= t, j >= c).astype(BF16)
    return pl.pallas_call(
        functools.partial(_sb_kernel, t=t, heads=heads, window=window),
        grid=(bsz, n_groups, s // rows),
        in_specs=[pl.BlockSpec((1, rows, w), lambda b, g, i: (b, i, g)),
                  pl.BlockSpec((1, s, w), lambda b, g, i: (b, 0, n_groups + g)),
                  pl.BlockSpec((1, s, w), lambda b, g, i: (b, 0, 2 * n_groups + g)),
                  pl.BlockSpec(u.shape, lambda b, g, i: (0, 0))],
        out_specs=pl.BlockSpec((1, rows, w), lambda b, g, i: (b, i, g)),
        out_shape=jax.ShapeDtypeStruct((bsz, s, ATTN_WIDTH), BF16),
        scratch_shapes=[pltpu.VMEM((heads, t, LANES), F32),
                        pltpu.VMEM((heads, t, LANES), F32)],
        compiler_params=_cparams(3),
        name="stickbreak_attention",
    )(qkv, qkv, qkv, u)


def _outproj_kernel(x_ref, mod_ref, yf_ref, ys_ref, wg_ref, wfp_ref, wsp_ref, wo_ref, g_ref, b_ref, o_ref):
    x = x_ref[0]
    d = x.shape[-1]
    sh = mod_ref[0, 0:1, :]
    sc = mod_ref[0, 1:2, :]
    gt = mod_ref[0, 2:3, :]
    u = (x * (1.0 + sc) + sh).astype(BF16)
    gates = jnp.dot(u, wg_ref[...], preferred_element_type=F32)
    pa = lax.dot_general(yf_ref[0], wfp_ref[...], (((0,), (0,)), ((), ())), preferred_element_type=F32)
    pb = jnp.dot(ys_ref[0], wsp_ref[...], preferred_element_type=F32)
    merged = jax.nn.sigmoid(gates[:, :d]) * pa + jax.nn.sigmoid(gates[:, d:]) * pb
    attn = jnp.dot(merged.astype(BF16), wo_ref[...], preferred_element_type=F32)
    y = DEEPNORM_ALPHA * x + (1.0 + gt) * attn
    o_ref[0] = _layer_norm(y, g_ref[...], b_ref[...])


def _outproj(x, mod, yf, ys, wg, wfp, wsp, wo, g, b, tm):
    bsz, s, d = x.shape
    aw = ys.shape[-1]
    return pl.pallas_call(
        _outproj_kernel,
        grid=(bsz, s // tm),
        in_specs=[pl.BlockSpec((1, tm, d), lambda b_, i: (b_, i, 0)),
                  pl.BlockSpec((1, N_MOD, d), lambda b_, i: (b_, 0, 0)),
                  pl.BlockSpec((1, aw, tm), lambda b_, i: (b_, 0, i)),
                  pl.BlockSpec((1, tm, aw), lambda b_, i: (b_, i, 0)),
                  _resident(wg.shape), _resident(wfp.shape), _resident(wsp.shape), _resident(wo.shape),
                  _resident(g.shape), _resident(b.shape)],
        out_specs=pl.BlockSpec((1, tm, d), lambda b_, i: (b_, i, 0)),
        out_shape=jax.ShapeDtypeStruct((bsz, s, d), F32),
        compiler_params=_cparams(2),
        name="merge_out_proj_ln",
    )(x, mod, yf, ys, wg, wfp, wsp, wo, g, b)


def _ffn_kernel(x_ref, mod_ref, wup_ref, cw_ref, cb_ref, wdn_ref, g_ref, b_ref, o_ref,
                tail_ref, hbuf_ref, *, tm, chunks):
    i = pl.program_id(1)
    d_ff = wdn_ref.shape[0]
    halo = SUBLANES

    @pl.when(i == 0)
    def _():
        tail_ref[...] = jnp.zeros(tail_ref.shape, F32)

    x = x_ref[0]
    sh = mod_ref[0, 3:4, :]
    sc = mod_ref[0, 4:5, :]
    gt = mod_ref[0, 5:6, :]
    u = (x * (1.0 + sc) + sh).astype(BF16)

    def up(col0, n):
        return jnp.dot(u, wup_ref[:, col0:col0 + n], preferred_element_type=F32)

    def conv(h, col0, n):
        hbuf_ref[0:halo, 0:n] = tail_ref[:, col0:col0 + n]
        hbuf_ref[halo:halo + tm, 0:n] = h
        tail_ref[:, col0:col0 + n] = h[tm - halo:tm, :]
        h1 = hbuf_ref[halo - 1:halo - 1 + tm, 0:n]
        h2 = hbuf_ref[halo - 2:halo - 2 + tm, 0:n]
        w = cw_ref[:, col0:col0 + n]
        out = cb_ref[:, col0:col0 + n] + h2 * w[0:1, :]
        out = out + h1 * w[1:2, :]
        return out + h * w[2:3, :]

    starts = [sum(chunks[:c]) for c in range(len(chunks))]
    ups = [None] * len(chunks)
    ups[0] = (up(starts[0], chunks[0]), up(d_ff + starts[0], chunks[0]))
    acc = jnp.zeros(x.shape, F32)
    for c, n in enumerate(chunks):
        if c + 1 < len(chunks):
            ups[c + 1] = (up(starts[c + 1], chunks[c + 1]), up(d_ff + starts[c + 1], chunks[c + 1]))
        hg = conv(ups[c][0], starts[c], n)
        hv = conv(ups[c][1], d_ff + starts[c], n)
        act = (hg * jax.nn.sigmoid(hg)) * hv
        acc = acc + jnp.dot(act.astype(BF16), wdn_ref[starts[c]:starts[c] + n, :],
                            preferred_element_type=F32)

    y = DEEPNORM_ALPHA * x + (1.0 + gt) * acc
    o_ref[0] = _layer_norm(y, g_ref[...], b_ref[...])


def _ffn(x, mod, wup, cw, cb, wdn, g, b, tm, chunks):
    bsz, s, d = x.shape
    f2 = wup.shape[1]
    assert sum(chunks) == wdn.shape[0] and cw.shape[0] == CONV_WIDTH and CONV_WIDTH - 1 <= SUBLANES
    return pl.pallas_call(
        functools.partial(_ffn_kernel, tm=tm, chunks=chunks),
        grid=(bsz, s // tm),
        in_specs=[pl.BlockSpec((1, tm, d), lambda b_, i: (b_, i, 0)),
                  pl.BlockSpec((1, N_MOD, d), lambda b_, i: (b_, 0, 0)),
                  _resident(wup.shape), _resident(cw.shape), _resident(cb.shape), _resident(wdn.shape),
                  _resident(g.shape), _resident(b.shape)],
        out_specs=pl.BlockSpec((1, tm, d), lambda b_, i: (b_, i, 0)),
        out_shape=jax.ShapeDtypeStruct((bsz, s, d), F32),
        scratch_shapes=[pltpu.VMEM((SUBLANES, f2), F32),
                        pltpu.VMEM((SUBLANES + tm, max(chunks)), F32)],
        compiler_params=_cparams(2),
        name="conv_ffn_ln",
    )(x, mod, wup, cw, cb, wdn, g, b)


def _tiles(s):
    return dict(
        proj_rows=min(512, s),
        fox_queries=min(2048, s),
        fox_keys=min(512, s),
        sb_block=LANES,
        sb_heads=N_HEADS,
        sb_window=3,
        sb_rows=min(1024, s),
        ffn_rows=min(512, s),
    )


def _ffn_chunks(d_ff):
    units = d_ff // MXU_WIDTH
    assert units * MXU_WIDTH == d_ff and units >= 5
    mid = units - 1 - 2 - 2
    return tuple(MXU_WIDTH * n for n in (1, 2, mid - mid // 2, mid // 2, 2))


def kernel(x, c, w_ada, b_ada, w_in, b_forget, w_fox_proj, w_sb_proj, w_o, ln1_g, ln1_b,
           w_up, conv_w, conv_b, w_down, ln2_g, ln2_b):
    bsz, s, d = x.shape
    d_ff = w_down.shape[0]
    tl = _tiles(s)
    aw = ATTN_WIDTH

    qscale = jnp.concatenate([jnp.full((aw,), ATTN_SCALE, F32), jnp.ones((2 * aw,), F32)])
    wa = (w_in[:, 0:3 * aw] * qscale).astype(BF16)
    wf = jnp.pad(w_in[:, 3 * aw:3 * aw + N_HEADS], ((0, 0), (0, LANES - N_HEADS))).astype(BF16)
    b0 = 3 * aw + N_HEADS
    wb = (w_in[:, b0:b0 + 3 * aw] * qscale).astype(BF16)
    wg = w_in[:, b0 + 3 * aw:].astype(BF16)

    for _ in range(DEPTH):
        mod = _ada(c, w_ada, b_ada).reshape(bsz, N_MOD, d)
        b_f = jnp.pad(b_forget, (0, LANES - N_HEADS)).reshape(1, LANES)
        qx, kx, vt, qn2, kn2, clast, qkvb = _inproj(x, mod, wa, wb, wf, b_f, tl["proj_rows"])
        tq, tk = tl["fox_queries"], tl["fox_keys"]
        assert tk == tl["proj_rows"] and tq % tk == 0
        qn2 = jnp.max(qn2[..., 0].reshape(bsz, s // tq, tq // tk, N_HEADS), axis=2)
        kn2 = jnp.max(kn2[..., 0], axis=1)
        qkmax = jnp.sqrt(jnp.transpose(qn2, (0, 2, 1)) * kn2[:, :, None])
        clast = jnp.transpose(clast[:, :, 0, :N_HEADS], (0, 2, 1))
        y_fox = _fox_attention(qx, kx, vt, clast.reshape(-1), qkmax.reshape(-1), tq, tk)
        y_fox = y_fox.reshape(bsz, ATTN_WIDTH, s)
        y_sb = _sb_attention(qkvb, tl["sb_block"], tl["sb_heads"], tl["sb_window"], tl["sb_rows"])
        x = _outproj(x, mod, y_fox, y_sb, wg, w_fox_proj.astype(BF16), w_sb_proj.astype(BF16),
                     w_o.astype(BF16), ln1_g.reshape(1, d), ln1_b.reshape(1, d), tl["proj_rows"])
        x = _ffn(x, mod, w_up.astype(BF16), conv_w, conv_b.reshape(1, 2 * d_ff), w_down.astype(BF16),
                 ln2_g.reshape(1, d), ln2_b.reshape(1, d), tl["ffn_rows"], _ffn_chunks(d_ff))
    return x
```
